```python
import math
import jax, jax.numpy as jnp
from jax import lax
import numpy as np

D_MODEL = 2048
BATCH = 8
SEQ = 2048
DEPTH = 4

GRID_W = 64
GDN_HEADS = 6
GDN_DK = 128
GDN_DV = 128
GDN_WIDTH = GDN_HEADS * GDN_DV
SSM_HEADS = 12
SSM_HEADDIM = 64
SSM_WIDTH = SSM_HEADS * SSM_HEADDIM
SSM_GROUPS = 2
SSM_HPG = SSM_HEADS // SSM_GROUPS
SSM_STATE = 128
NA_HEADS = 8
NA_HEADDIM = 64
NA_WIDTH = NA_HEADS * NA_HEADDIM
NA_KH = 8
NA_KW = 16
MIX_WIDTH = GDN_WIDTH + SSM_WIDTH + NA_WIDTH
CONV_K = 5
CHUNK = 64
D_FF = 5632
N_EXPERTS = 8
TOP_K = 2
EPS = 1e-6

CONV_SIZES = (GDN_WIDTH, GDN_WIDTH, GDN_WIDTH, SSM_WIDTH, SSM_GROUPS * SSM_STATE, SSM_GROUPS * SSM_STATE)
REST_SIZES = (GDN_WIDTH, 2 * GDN_HEADS, 2 * GDN_HEADS, SSM_WIDTH, 2 * SSM_HEADS, NA_WIDTH, NA_WIDTH, NA_WIDTH)
CONV_CH = GDN_WIDTH * 3 + SSM_WIDTH + 2 * SSM_GROUPS * SSM_STATE
IN_WIDTH = CONV_CH + 2 * GDN_WIDTH + 4 * GDN_HEADS + 2 * SSM_HEADS + 3 * NA_WIDTH - GDN_WIDTH + SSM_WIDTH

kernel_name = "hybrid_parallel_heads_encoder"


def rmsnorm(x, g):
    xf = x.astype(jnp.float32)
    y = xf * lax.rsqrt(jnp.mean(xf * xf, axis=-1, keepdims=True) + EPS)
    return (y * g.astype(jnp.float32)).astype(x.dtype)


def l2norm(x):
    return x * lax.rsqrt(jnp.sum(x * x, axis=-1, keepdims=True) + EPS)


def split_points(sizes):
    return tuple(int(s) for s in np.cumsum(sizes)[:-1])


def depthwise_conv_silu(x, w, b):
    ch = x.shape[-1]
    y = lax.conv_general_dilated(x, w[:, None, :].astype(x.dtype), window_strides=(1,),
                                 padding=[(CONV_K // 2, CONV_K // 2)],
                                 dimension_numbers=('NWC', 'WIO', 'NWC'), feature_group_count=ch)
    return jax.nn.silu(y + b)


def gdn_one_direction(q, k, v, g, beta):
    bsz, L, H, DK = q.shape
    DV = v.shape[-1]
    nc = L // CHUNK

    def chunks(t):
        t = t.reshape((bsz, nc, CHUNK, H) + t.shape[3:])
        return jnp.moveaxis(t, 3, 1)

    q, k, v, beta = chunks(q), chunks(k), chunks(v), chunks(beta)
    g = jnp.cumsum(chunks(g), axis=-1)
    tri_incl = jnp.tril(jnp.ones((CHUNK, CHUNK), dtype=bool))
    tri_strict = jnp.tril(jnp.ones((CHUNK, CHUNK), dtype=bool), -1)
    decay = jnp.exp(jnp.where(tri_incl, g[..., :, None] - g[..., None, :], -jnp.inf))
    kk = jnp.einsum('bhnck,bhnsk->bhncs', k, k)
    a_mat = jnp.where(tri_strict, beta[..., None] * kk * decay, 0.0)
    ia = a_mat + jnp.eye(CHUNK, dtype=q.dtype)
    rhs = jnp.concatenate([v * beta[..., None], k * (beta * jnp.exp(g))[..., None]], axis=-1)
    sol = lax.linalg.triangular_solve(ia, rhs, left_side=True, lower=True, unit_diagonal=True)
    value, kcd = sol[..., :DV], sol[..., DV:]
    qk = jnp.einsum('bhnck,bhnsk->bhncs', q, k) * decay
    q_dec = q * jnp.exp(g)[..., None]
    g_last = g[..., -1]
    k_dec = k * jnp.exp(g_last[..., None] - g)[..., None]

    def step(S, xs):
        value_c, kcd_c, qk_c, q_dec_c, k_dec_c, gl_c = xs
        v_new = value_c - jnp.einsum('bhck,bhkv->bhcv', kcd_c, S)
        o = jnp.einsum('bhck,bhkv->bhcv', q_dec_c, S) + jnp.einsum('bhcs,bhsv->bhcv', qk_c, v_new)
        S = S * jnp.exp(gl_c)[..., None, None] + jnp.einsum('bhck,bhcv->bhkv', k_dec_c, v_new)
        return S, o

    xs = tuple(jnp.moveaxis(t, 2, 0) for t in (value, kcd, qk, q_dec, k_dec, g_last))
    s0 = jnp.zeros((bsz, H, DK, DV), q.dtype)
    _, o = lax.scan(step, s0, xs)
    o = jnp.moveaxis(o, 0, 2)
    return jnp.moveaxis(o, 1, 3).reshape(bsz, L, H, DV)


def gdn_mixer(qc, kc, vc, z, b_raw, a_raw, A_log, dt_bias, norm_w):
    bsz, L, _ = qc.shape
    f32 = jnp.float32
    q = l2norm(qc.astype(f32).reshape(bsz, L, GDN_HEADS, GDN_DK)) * (GDN_DK ** -0.5)
    k = l2norm(kc.astype(f32).reshape(bsz, L, GDN_HEADS, GDN_DK))
    v = vc.astype(f32).reshape(bsz, L, GDN_HEADS, GDN_DV)
    beta = jax.nn.sigmoid(b_raw.astype(f32).reshape(bsz, L, 2, GDN_HEADS))
    g = -jnp.exp(A_log.astype(f32)) * jax.nn.softplus(
        a_raw.astype(f32).reshape(bsz, L, 2, GDN_HEADS) + dt_bias.astype(f32))
    flip = lambda t: jnp.flip(t, axis=1)
    o_f = gdn_one_direction(q, k, v, g[:, :, 0], beta[:, :, 0])
    o_b = flip(gdn_one_direction(flip(q), flip(k), flip(v), flip(g[:, :, 1]), flip(beta[:, :, 1])))
    o = rmsnorm(o_f + o_b, norm_w) * jax.nn.silu(z.astype(f32).reshape(bsz, L, GDN_HEADS, GDN_DV))
    return o.reshape(bsz, L, GDN_WIDTH).astype(qc.dtype)


def ssd_one_direction(x, dt, A, Bm, Cm):
    bsz, L = x.shape[:2]
    nc = L // CHUNK
    ch = lambda t: t.reshape((bsz, nc, CHUNK) + t.shape[2:])
    xdt = ch(x * dt[..., None])
    a = jnp.moveaxis(ch(dt * A), 2, -1)
    Bm, Cm = ch(Bm), ch(Cm)
    acum = jnp.cumsum(a, axis=-1)
    tri_incl = jnp.tril(jnp.ones((CHUNK, CHUNK), dtype=bool))
    lmat = jnp.exp(jnp.where(tri_incl, acum[..., :, None] - acum[..., None, :], -jnp.inf))
    cb = jnp.einsum('bclgn,bcsgn->bcgls', Cm, Bm)
    y_diag = jnp.einsum('bcghls,bcsghp->bclghp', cb[:, :, :, None] * lmat, xdt)
    decay_states = jnp.moveaxis(jnp.exp(acum[..., -1:] - acum), -1, 2)
    states = jnp.einsum('bcsgn,bcsghp->bcghpn', Bm, xdt * decay_states[..., None])
    chunk_decay = jnp.exp(acum[..., -1])

    def step(h, inp):
        st, dc = inp
        return h * dc[..., None, None] + st, h

    h0 = jnp.zeros(states.shape[:1] + states.shape[2:], x.dtype)
    _, prev = lax.scan(step, h0, (jnp.moveaxis(states, 1, 0), jnp.moveaxis(chunk_decay, 1, 0)))
    prev = jnp.moveaxis(prev, 0, 1)
    y_off = jnp.einsum('bclgn,bcghpn->bclghp', Cm, prev) * jnp.exp(jnp.moveaxis(acum, -1, 2))[..., None]
    return (y_diag + y_off).reshape(x.shape)


def ssd_mixer(xc, Bc, Cc, z, dt_raw, A_log, dt_bias, d_skip, norm_w):
    bsz, L, _ = xc.shape
    f32 = jnp.float32
    x = xc.astype(f32).reshape(bsz, L, SSM_GROUPS, SSM_HPG, SSM_HEADDIM)
    Bm = Bc.astype(f32).reshape(bsz, L, SSM_GROUPS, SSM_STATE)
    Cm = Cc.astype(f32).reshape(bsz, L, SSM_GROUPS, SSM_STATE)
    dt = jax.nn.softplus(dt_raw.astype(f32).reshape(bsz, L, 2, SSM_HEADS) + dt_bias.astype(f32))
    dt = dt.reshape(bsz, L, 2, SSM_GROUPS, SSM_HPG)
    A = -jnp.exp(A_log.astype(f32)).reshape(2, SSM_GROUPS, SSM_HPG)
    flip = lambda t: jnp.flip(t, axis=1)
    y_f = ssd_one_direction(x, dt[:, :, 0], A[0], Bm, Cm)
    y_b = flip(ssd_one_direction(flip(x), flip(dt[:, :, 1]), A[1], flip(Bm), flip(Cm)))
    y = y_f + y_b + x * d_skip.astype(f32).reshape(SSM_GROUPS, SSM_HPG)[..., None]
    y = y.reshape(bsz, L, SSM_WIDTH) * jax.nn.silu(z.astype(f32))
    return rmsnorm(y, norm_w).astype(xc.dtype)


def na_mixer(q, k, v, rpb):
    bsz, L, _ = q.shape
    rows = L // GRID_W
    kh = min(NA_KH, rows)
    r = np.arange(rows)
    row_start = np.clip(r - kh // 2, 0, rows - kh)
    row_idx = row_start[:, None] + np.arange(kh)[None, :]
    dr_idx = (row_idx - r[:, None]) + NA_KH - 1
    cols = np.arange(GRID_W)
    col_start = np.clip(cols - NA_KW // 2, 0, GRID_W - NA_KW)
    in_win = (cols[None, :] >= col_start[:, None]) & (cols[None, :] < col_start[:, None] + NA_KW)
    dc_idx = np.clip(cols[None, :] - cols[:, None], -(NA_KW - 1), NA_KW - 1) + NA_KW - 1
    bias = rpb[:, dr_idx[:, None, :, None], dc_idx[None, :, None, :]].astype(jnp.float32)
    bias = jnp.where(in_win[None, None, :, None, :], bias, -jnp.inf)
    qg = q.reshape(bsz, rows, GRID_W, NA_HEADS, NA_HEADDIM)
    kg = k.reshape(bsz, rows, GRID_W, NA_HEADS, NA_HEADDIM)
    vg = v.reshape(bsz, rows, GRID_W, NA_HEADS, NA_HEADDIM)
    ridx = jnp.asarray(row_idx, dtype=jnp.int32)
    k_rows = jnp.take(kg, ridx, axis=1)
    v_rows = jnp.take(vg, ridx, axis=1)
    s = jnp.einsum('brqhd,brikhd->bhrqik', qg, k_rows).astype(jnp.float32) * (NA_HEADDIM ** -0.5) + bias[None]
    p = jax.nn.softmax(s.reshape(bsz, NA_HEADS, rows, GRID_W, kh * GRID_W), axis=-1)
    p = p.reshape(s.shape).astype(v.dtype)
    o = jnp.einsum('bhrqik,brikhd->brqhd', p, v_rows)
    return o.reshape(bsz, L, NA_WIDTH)


def swiglu(h, w1, w3, w2):
    return (jax.nn.silu(h @ w1) * (h @ w3)) @ w2


def moe_ffn(h, router, w1, w3, w2):
    bsz, L, D = h.shape
    t = h.reshape(-1, D)
    logits = (t @ router).astype(jnp.float32)
    top_val, top_idx = lax.top_k(logits, TOP_K)
    wts = jax.nn.softmax(top_val, axis=-1)
    gates = jnp.sum(jax.nn.one_hot(top_idx, N_EXPERTS, dtype=jnp.float32) * wts[..., None], axis=1).astype(t.dtype)
    out = jnp.zeros_like(t)
    for e in range(N_EXPERTS):
        out = out + gates[:, e:e + 1] * swiglu(t, w1[e], w3[e], w2[e])
    return out.reshape(bsz, L, D)


def setup_inputs(seed: int = 0) -> dict:
    key = jax.random.key(seed)
    ks = iter(jax.random.split(key, 40))
    f32 = jnp.float32
    nrm = lambda shape, s: jax.random.normal(next(ks), shape, f32) * s

    def a_log(shape):
        return jnp.log(jax.random.uniform(next(ks), shape, f32, 1.0, 16.0))

    def dt_bias(shape):
        dt = jnp.exp(jax.random.uniform(next(ks), shape, f32, math.log(1e-3), math.log(1e-1)))
        return dt + jnp.log(-jnp.expm1(-dt))

    n_dense = (DEPTH + 1) // 2
    n_moe = DEPTH // 2
    D = D_MODEL
    return {
        "x": nrm((BATCH, SEQ, D), 1.0),
        "c": nrm((BATCH, D), 1.0),
        "ada_w": nrm((DEPTH, D, 6 * D), 0.5 * D ** -0.5),
        "ada_b": nrm((DEPTH, 6 * D), 0.01),
        "norm_mix": 1.0 + nrm((DEPTH, D), 0.02),
        "norm_ffn": 1.0 + nrm((DEPTH, D), 0.02),
        "norm_final": 1.0 + nrm((D,), 0.02),
        "w_in": nrm((DEPTH, D, IN_WIDTH), D ** -0.5),
        "conv_w": nrm((DEPTH, CONV_K, CONV_CH), CONV_K ** -0.5),
        "conv_b": nrm((DEPTH, CONV_CH), 0.01),
        "gdn_A_log": a_log((DEPTH, 2, GDN_HEADS)),
        "gdn_dt_bias": dt_bias((DEPTH, 2, GDN_HEADS)),
        "gdn_norm": 1.0 + nrm((DEPTH, GDN_DV), 0.02),
        "ssm_A_log": a_log((DEPTH, 2, SSM_HEADS)),
        "ssm_dt_bias": dt_bias((DEPTH, 2, SSM_HEADS)),
        "ssm_D": 1.0 + nrm((DEPTH, SSM_HEADS), 0.1),
        "ssm_norm": 1.0 + nrm((DEPTH, SSM_WIDTH), 0.02),
        "na_rpb": nrm((DEPTH, NA_HEADS, 2 * NA_KH - 1, 2 * NA_KW - 1), 0.02),
        "w_out": nrm((DEPTH, MIX_WIDTH, D), MIX_WIDTH ** -0.5),
        "ffn_w1": nrm((n_dense, D, D_FF), D ** -0.5),
        "ffn_w3": nrm((n_dense, D, D_FF), D ** -0.5),
        "ffn_w2": nrm((n_dense, D_FF, D), D_FF ** -0.5),
        "moe_router": nrm((n_moe, D, N_EXPERTS), D ** -0.5),
        "moe_w1": nrm((n_moe, N_EXPERTS, D, D_FF), D ** -0.5),
        "moe_w3": nrm((n_moe, N_EXPERTS, D, D_FF), D ** -0.5),
        "moe_w2": nrm((n_moe, N_EXPERTS, D_FF, D), D_FF ** -0.5),
    }


def reference(x, c, ada_w, ada_b, norm_mix, norm_ffn, norm_final, w_in, conv_w, conv_b,
              gdn_A_log, gdn_dt_bias, gdn_norm, ssm_A_log, ssm_dt_bias, ssm_D, ssm_norm,
              na_rpb, w_out, ffn_w1, ffn_w3, ffn_w2, moe_router, moe_w1, moe_w3, moe_w2):
    conv_pts = split_points(CONV_SIZES)
    rest_pts = split_points(REST_SIZES)
    cond = jax.nn.silu(c)
    for l in range(DEPTH):
        mod = (cond @ ada_w[l] + ada_b[l])[:, None, :]
        sh1, sc1, g1, sh2, sc2, g2 = jnp.split(mod, 6, axis=-1)
        h = rmsnorm(x, norm_mix[l]) * (1.0 + sc1) + sh1
        proj = h @ w_in[l]
        conv_part = depthwise_conv_silu(proj[..., :CONV_CH], conv_w[l], conv_b[l])
        gq, gk, gv, sx, sB, sC = jnp.split(conv_part, conv_pts, axis=-1)
        gz, gb, ga, sz, sdt, nq, nk, nv = jnp.split(proj[..., CONV_CH:], rest_pts, axis=-1)
        ya = gdn_mixer(gq, gk, gv, gz, gb, ga, gdn_A_log[l], gdn_dt_bias[l], gdn_norm[l])
        yb = ssd_mixer(sx, sB, sC, sz, sdt, ssm_A_log[l], ssm_dt_bias[l], ssm_D[l], ssm_norm[l])
        yc = na_mixer(nq, nk, nv, na_rpb[l])
        y = jnp.concatenate([ya, yb, yc], axis=-1) @ w_out[l]
        x = x + g1 * y
        h2 = rmsnorm(x, norm_ffn[l]) * (1.0 + sc2) + sh2
        if l % 2 == 0:
            f = swiglu(h2, ffn_w1[l // 2], ffn_w3[l // 2], ffn_w2[l // 2])
        else:
            f = moe_ffn(h2, moe_router[l // 2], moe_w1[l // 2], moe_w3[l // 2], moe_w2[l // 2])
        x = x + g2 * f
    return rmsnorm(x, norm_final)
```

```python
import functools
import math

import jax
import jax.numpy as jnp
import numpy as np
from jax import lax
from jax.experimental import pallas as pl
from jax.experimental.pallas import tpu as pltpu

f32 = jnp.float32
bf16 = jnp.bfloat16
i32 = jnp.int32

GRID_W = 64
GDN_HEADS = 6
GDN_DK = 128
GDN_WIDTH = 768
SSM_HEADS = 12
SSM_HEADDIM = 64
SSM_WIDTH = 768
SSM_GROUPS = 2
SSM_HPG = 6
SSM_STATE = 128
NA_HEADS = 8
NA_HEADDIM = 64
NA_WIDTH = 512
NA_KH = 8
NA_KW = 16
CONV_K = 5
CONV_CH = 3584
N_EXPERTS = 8
EPS = 1e-6

OFF_GQ, OFF_GK, OFF_GV, OFF_SX, OFF_SZ, OFF_GZ = 0, 768, 1536, 2304, 3072, 3840
OFF_SB, OFF_SC, OFF_NQ, OFF_NK, OFF_NV, OFF_SM = 4608, 4864, 5120, 5632, 6144, 6656
PROJ_W = 6912
SM_BETA, SM_GA, SM_DT = 0, 12, 24

LANE = 128
MIX_CHUNK = 128
CONV_HALO = 8
MIB = 1024 * 1024


def _cparams(sem, vmem_mib):
    return pltpu.CompilerParams(dimension_semantics=sem, vmem_limit_bytes=vmem_mib * MIB)


def _sigmoid(x):
    return 1.0 / (1.0 + jnp.exp(-x))


def _softplus(x):
    return jnp.maximum(x, 0.0) + jnp.log1p(jnp.exp(-jnp.abs(x)))


def _dot(a, b):
    return jnp.dot(a.astype(bf16), b.astype(bf16), preferred_element_type=f32)


def _dot_nt(a, b):
    return lax.dot_general(a.astype(bf16), b.astype(bf16), (((1,), (1,)), ((), ())),
                           preferred_element_type=f32)


def _split3(x):
    hi = x.astype(bf16)
    r1 = x - hi.astype(f32)
    mid = r1.astype(bf16)
    lo = (r1 - mid.astype(f32)).astype(bf16)
    return hi, mid, lo


def _dot3(a, b):
    ah = a.astype(bf16)
    al = (a - ah.astype(f32)).astype(bf16)
    bh = b.astype(bf16)
    bl = (b - bh.astype(f32)).astype(bf16)
    return (jnp.dot(ah, bh, preferred_element_type=f32) + jnp.dot(ah, bl, preferred_element_type=f32)
            + jnp.dot(al, bh, preferred_element_type=f32))


def _sel_left(m01, x):
    m = m01.astype(bf16)
    return sum(jnp.dot(m, t, preferred_element_type=f32) for t in _split3(x))


def _sel_right(x, m01):
    m = m01.astype(bf16)
    return sum(jnp.dot(t, m, preferred_element_type=f32) for t in _split3(x))


def _rms(x):
    return x * lax.rsqrt(jnp.mean(x * x, axis=-1, keepdims=True) + EPS)


def _conv_silu(xp_ref, cw_ref, cb_ref, r0, rows):
    n = rows + 2 * CONV_HALO
    blk = xp_ref[pl.ds(r0, n), :]
    acc = jnp.broadcast_to(cb_ref[...], (rows, blk.shape[1]))
    for j in range(CONV_K):
        shift = CONV_HALO - CONV_K // 2 + j
        acc = acc + cw_ref[j:j + 1, :] * pltpu.roll(blk, n - shift, axis=0)[0:rows]
    return acc * _sigmoid(acc)


def _fill_padded(src_ref, dst_ref, L):
    w = dst_ref.shape[1]
    dst_ref[0:CONV_HALO, :] = jnp.zeros((CONV_HALO, w), f32)
    dst_ref[L + CONV_HALO:L + 2 * CONV_HALO, :] = jnp.zeros((CONV_HALO, w), f32)
    dst_ref[CONV_HALO:L + CONV_HALO, :] = src_ref[0]


def _ada_kernel(c_ref, w_ref, b_ref, o_ref):
    c = c_ref[...]
    cond = c * _sigmoid(c)
    o_ref[0] = _dot(cond, w_ref[0]) + b_ref[0]


def _ada_mod(c, ada_w, ada_b):
    depth, d, n = ada_w.shape
    bsz = c.shape[0]
    tn = 1536
    return pl.pallas_call(
        _ada_kernel,
        out_shape=jax.ShapeDtypeStruct((depth, bsz, n), f32),
        grid=(depth, n // tn),
        in_specs=[pl.BlockSpec((bsz, d), lambda l, j: (0, 0)),
                  pl.BlockSpec((1, d, tn), lambda l, j: (l, 0, j)),
                  pl.BlockSpec((1, 1, tn), lambda l, j: (l, 0, j))],
        out_specs=pl.BlockSpec((1, bsz, tn), lambda l, j: (l, 0, j)),
        compiler_params=_cparams(("arbitrary", "arbitrary"), 40),
        name="ada_mod",
    )(c, ada_w, ada_b.reshape(depth, 1, n))


NORM_ROWS = 128


def _norm_mod_rows(x_ref, nw, sc, sh, h_ref):
    def body(i, carry):
        r = pl.multiple_of(i * NORM_ROWS, NORM_ROWS)
        y = _rms(x_ref[pl.ds(r, NORM_ROWS), :]) * nw
        h_ref[pl.ds(r, NORM_ROWS), :] = (y * (1.0 + sc) + sh).astype(h_ref.dtype)
        return carry
    lax.fori_loop(0, x_ref.shape[0] // NORM_ROWS, body, 0)


def _inproj_kernel(x_ref, mod_ref, nw_ref, w_ref, o_ref, h_ref):
    @pl.when(pl.program_id(1) == 0)
    def _():
        _norm_mod_rows(x_ref, nw_ref[...], mod_ref[0, 1:2, :], mod_ref[0, 0:1, :], h_ref)
    o_ref[...] = jnp.dot(h_ref[...], w_ref[...], preferred_element_type=f32)


def _inproj(x2, mod_l, nw, w, L):
    T, D = x2.shape
    NP = w.shape[1]
    tm, tn = min(1024, L), 768
    per_b = L // tm
    return pl.pallas_call(
        _inproj_kernel,
        out_shape=jax.ShapeDtypeStruct((T, NP), f32),
        grid=(T // tm, NP // tn),
        in_specs=[pl.BlockSpec((tm, D), lambda i, j: (i, 0)),
                  pl.BlockSpec((1, 6, D), lambda i, j: (i // per_b, 0, 0)),
                  pl.BlockSpec((1, D), lambda i, j: (0, 0)),
                  pl.BlockSpec((D, tn), lambda i, j: (0, j))],
        out_specs=pl.BlockSpec((tm, tn), lambda i, j: (i, j)),
        scratch_shapes=[pltpu.VMEM((tm, D), bf16)],
        compiler_params=_cparams(("arbitrary", "arbitrary"), 44),
        name="inproj",
    )(x2, mod_l, nw, w)


def _gdn_kernel(q_ref, k_ref, v_ref, z_ref, sm_ref, cwq, cwk, cwv, cbq, cbk, cbv,
                alog_ref, dtb_ref, nw_ref, o_ref,
                qp, kp, vp, val_s, kcd_s, qd_s, kdt_s, qk_s, gl_s, oacc):
    h = pl.program_id(1)
    L = q_ref.shape[1]
    C = MIX_CHUNK
    nc = L // C
    _fill_padded(q_ref, qp, L)
    _fill_padded(k_ref, kp, L)
    _fill_padded(v_ref, vp, L)
    oacc[...] = jnp.zeros(oacc.shape, f32)

    lane = lax.broadcasted_iota(i32, (C, LANE), 1)
    ri = lax.broadcasted_iota(i32, (C, C), 0)
    ci = lax.broadcasted_iota(i32, (C, C), 1)
    tril = ri >= ci
    eye = jnp.where(ri == ci, 1.0, 0.0)
    trilf = jnp.where(tril, 1.0, 0.0)
    neg_a = -jnp.exp(alog_ref[...])
    dtb = dtb_ref[...]

    def col(blk, idx):
        return jnp.sum(jnp.where(lane[0:blk.shape[0]] == idx, blk, 0.0), axis=-1, keepdims=True)

    def phase_a(c, carry):
        r0 = pl.multiple_of(c * C, C)
        q = _conv_silu(qp, cwq, cbq, r0, C)
        k = _conv_silu(kp, cwk, cbk, r0, C)
        v = _conv_silu(vp, cwv, cbv, r0, C)
        qn = q * lax.rsqrt(jnp.sum(q * q, axis=-1, keepdims=True) + EPS) * (GDN_DK ** -0.5)
        kn = k * lax.rsqrt(jnp.sum(k * k, axis=-1, keepdims=True) + EPS)
        sm = sm_ref[0, pl.ds(r0, C), :]
        beta_blk = _sigmoid(sm)
        g_blk = neg_a * _softplus(sm + dtb)
        cum = _sel_left(trilf, g_blk)
        tot = cum[C - 1:C, :]
        rev = tot - cum + g_blk
        kk = _dot_nt(kn, kn)
        qk = _dot_nt(qn, kn)
        for d in (0, 1):
            gidx = SM_GA + d * GDN_HEADS + h
            beta = col(beta_blk, SM_BETA + d * GDN_HEADS + h)
            gcol = col(cum if d == 0 else rev, gidx)
            gl = col(tot, gidx)
            G = jnp.broadcast_to(gcol, (C, C))
            diff = G - G.T
            mask = tril if d == 0 else ri <= ci
            strict = (ri > ci) if d == 0 else (ri < ci)
            dec = jnp.exp(jnp.where(mask, diff, -jnp.inf))
            a_mat = jnp.where(strict, beta * kk * dec, 0.0)
            p = a_mat
            t_inv = eye - a_mat
            for _ in range(int(math.log2(C)) - 1):
                p = _dot3(p, p)
                t_inv = t_inv + _dot3(t_inv, p)
            eg = jnp.exp(gcol)
            rhs = jnp.concatenate([v * beta, kn * (beta * eg)], axis=1)
            sol = _dot(t_inv, rhs)
            val_s[d, pl.ds(r0, C), :] = sol[:, :LANE]
            kcd_s[d, pl.ds(r0, C), :] = sol[:, LANE:].astype(bf16)
            qd_s[d, pl.ds(r0, C), :] = (qn * eg).astype(bf16)
            kdt_s[d, c] = (kn * jnp.exp(gl - gcol)).T.astype(bf16)
            qk_s[d, pl.ds(r0, C), :] = (qk * dec).astype(bf16)
            gl_s[d, c] = jnp.broadcast_to(jnp.exp(gl), (8, LANE))
        return carry

    lax.fori_loop(0, nc, phase_a, 0)

    def phase_b(i, carry):
        out = []
        for d in (0, 1):
            s = carry[d]
            c = i if d == 0 else nc - 1 - i
            r0 = pl.multiple_of(c * C, C)
            sb = s.astype(bf16)
            v_new = val_s[d, pl.ds(r0, C), :] - jnp.dot(kcd_s[d, pl.ds(r0, C), :], sb,
                                                         preferred_element_type=f32)
            vb = v_new.astype(bf16)
            o = (jnp.dot(qd_s[d, pl.ds(r0, C), :], sb, preferred_element_type=f32)
                 + jnp.dot(qk_s[d, pl.ds(r0, C), :], vb, preferred_element_type=f32))
            oacc[pl.ds(r0, C), :] += o
            out.append(s * gl_s[d, c][0:1, :] + jnp.dot(kdt_s[d, c], vb, preferred_element_type=f32))
        return tuple(out)

    s0 = jnp.zeros((GDN_DK, LANE), f32)
    lax.fori_loop(0, nc, phase_b, (s0, s0))

    def fin(c, carry):
        r0 = pl.multiple_of(c * C, C)
        y = _rms(oacc[pl.ds(r0, C), :]) * nw_ref[...]
        z = z_ref[0, pl.ds(r0, C), :]
        o_ref[0, pl.ds(r0, C), :] = (y * (z * _sigmoid(z))).astype(o_ref.dtype)
        return carry

    lax.fori_loop(0, nc, fin, 0)


def _gdn_mixer(proj, conv_w, conv_b, alog_row, dtb_row, gdn_norm):
    bsz, L, _ = proj.shape
    C = MIX_CHUNK
    nc = L // C
    H = GDN_HEADS
    blk = lambda off: pl.BlockSpec((1, L, LANE), lambda b, h, o=off // LANE: (b, 0, o + h))
    cwb = lambda off: pl.BlockSpec((CONV_K, LANE), lambda b, h, o=off // LANE: (0, o + h))
    cbb = lambda off: pl.BlockSpec((1, LANE), lambda b, h, o=off // LANE: (0, o + h))
    row = pl.BlockSpec((1, LANE), lambda b, h: (0, 0))
    return pl.pallas_call(
        _gdn_kernel,
        out_shape=jax.ShapeDtypeStruct((bsz, L, GDN_WIDTH), bf16),
        grid=(bsz, H),
        in_specs=[blk(OFF_GQ), blk(OFF_GK), blk(OFF_GV), blk(OFF_GZ),
                  pl.BlockSpec((1, L, LANE), lambda b, h: (b, 0, OFF_SM // LANE)),
                  cwb(0), cwb(768), cwb(1536), cbb(0), cbb(768), cbb(1536),
                  row, row, row],
        out_specs=pl.BlockSpec((1, L, LANE), lambda b, h: (b, 0, h)),
        scratch_shapes=[pltpu.VMEM((L + 2 * CONV_HALO, LANE), f32)] * 3 + [
            pltpu.VMEM((2, L, LANE), f32),
            pltpu.VMEM((2, L, LANE), bf16),
            pltpu.VMEM((2, L, LANE), bf16),
            pltpu.VMEM((2, nc, LANE, C), bf16),
            pltpu.VMEM((2, L, C), bf16),
            pltpu.VMEM((2, nc, 8, LANE), f32),
            pltpu.VMEM((L, LANE), f32)],
        compiler_params=_cparams(("arbitrary", "arbitrary"), 40),
        name="gdn_mixer",
    )(proj, proj, proj, proj, proj, conv_w, conv_w, conv_w, conv_b, conv_b, conv_b,
      alog_row, dtb_row, gdn_norm)


SSD_W = SSM_HPG * SSM_HEADDIM


def _ssd_kernel(x_ref, z_ref, b_ref, c_ref, sm_ref, cwx, cwb, cwc, cbx, cbb, cbc,
                alog_ref, dtb_ref, dsk_ref, o_ref, xp, bp, cp, xc, bc, cc, yacc):
    g = pl.program_id(1)
    L = x_ref.shape[1]
    C = MIX_CHUNK
    nc = L // C
    _fill_padded(x_ref, xp, L)
    _fill_padded(b_ref, bp, L)
    _fill_padded(c_ref, cp, L)
    yacc[...] = jnp.zeros(yacc.shape, f32)

    def conv_all(c, carry):
        r0 = pl.multiple_of(c * C, C)
        xc[pl.ds(r0, C), :] = _conv_silu(xp, cwx, cbx, r0, C)
        bc[pl.ds(r0, C), :] = _conv_silu(bp, cwb, cbb, r0, C)
        cc[pl.ds(r0, C), :] = _conv_silu(cp, cwc, cbc, r0, C)
        return carry

    lax.fori_loop(0, nc, conv_all, 0)

    lane = lax.broadcasted_iota(i32, (C, LANE), 1)
    ri = lax.broadcasted_iota(i32, (C, C), 0)
    ci = lax.broadcasted_iota(i32, (C, C), 1)
    tril = ri >= ci
    trilf = jnp.where(tril, 1.0, 0.0)
    er = lax.broadcasted_iota(i32, (LANE, SSD_W), 0)
    el = lax.shift_right_logical(lax.broadcasted_iota(i32, (LANE, SSD_W), 1),
                                 int(math.log2(SSM_HEADDIM)))
    half = lax.broadcasted_iota(i32, (C, LANE), 1) < SSM_HEADDIM
    neg_a = -jnp.exp(alog_ref[...])
    dtb = dtb_ref[...]

    def col(blk, idx):
        return jnp.sum(jnp.where(lane == idx, blk, 0.0), axis=-1, keepdims=True)

    def step(i, carry):
        out = []
        for d in (0, 1):
            ht = carry[d]
            c = i if d == 0 else nc - 1 - i
            r0 = pl.multiple_of(c * C, C)
            base = SM_DT + d * SSM_HEADS + g * SSM_HPG
            expand = jnp.where(er == base + el, 1.0, 0.0)
            sm = sm_ref[0, pl.ds(r0, C), :]
            dt_blk = _softplus(sm + dtb)
            a_blk = neg_a * dt_blk
            cum = _sel_left(trilf, a_blk)
            tot = cum[C - 1:C, :]
            gsel = cum if d == 0 else tot - cum + a_blk
            g6 = _sel_right(gsel, expand)
            dt6 = _sel_right(dt_blk, expand)
            tot6 = _sel_right(jnp.broadcast_to(tot, (8, LANE)), expand)[0:1, :]
            xdt = xc[pl.ds(r0, C), :] * dt6
            bm = bc[pl.ds(r0, C), :]
            cm = cc[pl.ds(r0, C), :]
            cb = _dot_nt(cm, bm)
            mask = tril if d == 0 else ri <= ci
            pieces = []
            for pr in range(SSM_HPG // 2):
                xpair = xdt[:, pr * LANE:(pr + 1) * LANE].astype(bf16)
                ys = []
                for e in (0, 1):
                    gcol = col(gsel, base + 2 * pr + e)
                    G = jnp.broadcast_to(gcol, (C, C))
                    lm = jnp.exp(jnp.where(mask, G - G.T, -jnp.inf))
                    ys.append(jnp.dot((cb * lm).astype(bf16), xpair, preferred_element_type=f32))
                pieces.append(jnp.where(half, ys[0], ys[1]))
            y_diag = jnp.concatenate(pieces, axis=1)
            y_off = _dot(cm, ht) * jnp.exp(g6)
            yacc[pl.ds(r0, C), :] += y_diag + y_off
            st = _dot(bm.T, xdt * jnp.exp(tot6 - g6))
            out.append(ht * jnp.exp(tot6) + st)
        return tuple(out)

    h0 = jnp.zeros((SSM_STATE, SSD_W), f32)
    lax.fori_loop(0, nc, step, (h0, h0))

    def fin(c, carry):
        r0 = pl.multiple_of(c * C, C)
        y = yacc[pl.ds(r0, C), :] + xc[pl.ds(r0, C), :] * dsk_ref[...]
        z = z_ref[0, pl.ds(r0, C), :]
        o_ref[0, pl.ds(r0, C), :] = y * (z * _sigmoid(z))
        return carry

    lax.fori_loop(0, nc, fin, 0)


def _ssd_mixer(proj, conv_w, conv_b, alog_row, dtb_row, dskip_row):
    bsz, L, _ = proj.shape
    wide = lambda off: pl.BlockSpec((1, L, SSD_W), lambda b, g, o=off // SSD_W: (b, 0, o + g))
    nar = lambda off: pl.BlockSpec((1, L, LANE), lambda b, g, o=off // LANE: (b, 0, o + g))
    row = pl.BlockSpec((1, LANE), lambda b, g: (0, 0))
    return pl.pallas_call(
        _ssd_kernel,
        out_shape=jax.ShapeDtypeStruct((bsz, L, SSM_WIDTH), f32),
        grid=(bsz, SSM_GROUPS),
        in_specs=[wide(OFF_SX), wide(OFF_SZ), nar(OFF_SB), nar(OFF_SC),
                  pl.BlockSpec((1, L, LANE), lambda b, g: (b, 0, OFF_SM // LANE)),
                  pl.BlockSpec((CONV_K, SSD_W), lambda b, g: (0, 2304 // SSD_W + g)),
                  pl.BlockSpec((CONV_K, LANE), lambda b, g: (0, 3072 // LANE + g)),
                  pl.BlockSpec((CONV_K, LANE), lambda b, g: (0, 3328 // LANE + g)),
                  pl.BlockSpec((1, SSD_W), lambda b, g: (0, 2304 // SSD_W + g)),
                  pl.BlockSpec((1, LANE), lambda b, g: (0, 3072 // LANE + g)),
                  pl.BlockSpec((1, LANE), lambda b, g: (0, 3328 // LANE + g)),
                  row, row,
                  pl.BlockSpec((1, SSD_W), lambda b, g: (0, g))],
        out_specs=pl.BlockSpec((1, L, SSD_W), lambda b, g: (b, 0, g)),
        scratch_shapes=[pltpu.VMEM((L + 2 * CONV_HALO, SSD_W), f32),
                        pltpu.VMEM((L + 2 * CONV_HALO, LANE), f32),
                        pltpu.VMEM((L + 2 * CONV_HALO, LANE), f32),
                        pltpu.VMEM((L, SSD_W), f32),
                        pltpu.VMEM((L, LANE), f32),
                        pltpu.VMEM((L, LANE), f32),
                        pltpu.VMEM((L, SSD_W), f32)],
        compiler_params=_cparams(("arbitrary", "arbitrary"), 48),
        name="ssd_mixer",
    )(proj, proj, proj, proj, proj, conv_w, conv_w, conv_w, conv_b, conv_b, conv_b,
      alog_row, dtb_row, dskip_row)


NA_WIN = NA_KH * GRID_W


def _na_kernel(q_ref, k_ref, v_ref, bias_ref, o_ref, kb, vb):
    L = q_ref.shape[1]
    rows = L // GRID_W
    kb[...] = k_ref[0].astype(bf16)
    vb[...] = v_ref[0].astype(bf16)
    low = lax.broadcasted_iota(i32, (GRID_W, LANE), 1) < NA_HEADDIM

    def body(r, carry):
        rs = jnp.clip(r - NA_KH // 2, 0, rows - NA_KH)
        off = r - rs
        q = q_ref[0, pl.ds(pl.multiple_of(r * GRID_W, GRID_W), GRID_W), :] * (NA_HEADDIM ** -0.5)
        w0 = pl.multiple_of(rs * GRID_W, GRID_W)
        kw = kb[pl.ds(w0, NA_WIN), :]
        vw = vb[pl.ds(w0, NA_WIN), :]
        outs = []
        for e in (0, 1):
            qm = jnp.where(low if e == 0 else jnp.logical_not(low), q, 0.0)
            s = _dot_nt(qm, kw) + bias_ref[e, off]
            p = jnp.exp(s - jnp.max(s, axis=-1, keepdims=True))
            den = jnp.sum(p, axis=-1, keepdims=True)
            outs.append(jnp.dot(p.astype(bf16), vw, preferred_element_type=f32) / den)
        o_ref[0, pl.ds(pl.multiple_of(r * GRID_W, GRID_W), GRID_W), :] = (
            jnp.where(low, outs[0], outs[1]).astype(o_ref.dtype))
        return carry

    lax.fori_loop(0, rows, body, 0)


def _na_bias_table(rpb):
    o = np.arange(NA_KH)[:, None]
    i = np.arange(NA_KH)[None, :]
    dr_idx = (i - o) + NA_KH - 1
    cols = np.arange(GRID_W)
    col_start = np.clip(cols - NA_KW // 2, 0, GRID_W - NA_KW)
    in_win = (cols[None, :] >= col_start[:, None]) & (cols[None, :] < col_start[:, None] + NA_KW)
    dc_idx = np.clip(cols[None, :] - cols[:, None], -(NA_KW - 1), NA_KW - 1) + NA_KW - 1
    bias = rpb[:, dr_idx[:, None, :, None], dc_idx[None, :, None, :]].astype(f32)
    bias = jnp.where(in_win[None, None, :, None, :], bias, -jnp.inf)
    return bias.reshape(NA_HEADS, NA_KH, GRID_W, NA_WIN)


def _na_mixer(proj, bias_tab):
    bsz, L, _ = proj.shape
    assert L // GRID_W >= NA_KH
    blk = lambda off: pl.BlockSpec((1, L, LANE), lambda b, p, o=off // LANE: (b, 0, o + p))
    return pl.pallas_call(
        _na_kernel,
        out_shape=jax.ShapeDtypeStruct((bsz, L, NA_WIDTH), bf16),
        grid=(bsz, NA_HEADS // 2),
        in_specs=[blk(OFF_NQ), blk(OFF_NK), blk(OFF_NV),
                  pl.BlockSpec((2, NA_KH, GRID_W, NA_WIN), lambda b, p: (p, 0, 0, 0))],
        out_specs=pl.BlockSpec((1, L, LANE), lambda b, p: (b, 0, p)),
        scratch_shapes=[pltpu.VMEM((L, LANE), bf16), pltpu.VMEM((L, LANE), bf16)],
        compiler_params=_cparams(("arbitrary", "arbitrary"), 32),
        name="na_mixer",
    )(proj, proj, proj, bias_tab)


def _outproj_kernel(ya_ref, yb_ref, yc_ref, sn_ref, wa_ref, wb_ref, wc_ref, x_ref, mod_ref, nw_ref,
                    xo_ref, h2_ref, ybn):
    _norm_mod_rows(yb_ref, sn_ref[...], 0.0, 0.0, ybn)
    y = (jnp.dot(ya_ref[...], wa_ref[...], preferred_element_type=f32)
         + jnp.dot(ybn[...], wb_ref[...], preferred_element_type=f32)
         + jnp.dot(yc_ref[...], wc_ref[...], preferred_element_type=f32))
    xo_ref[...] = x_ref[...] + mod_ref[0, 2:3, :] * y
    _norm_mod_rows(xo_ref, nw_ref[...], mod_ref[0, 4:5, :], mod_ref[0, 3:4, :], h2_ref)


def _outproj(ya, yb, yc, ssm_norm, wa, wb, wc, x2, mod_l, nw, L, h2_dtype):
    T, D = x2.shape
    tm = 512
    per_b = L // tm
    const = lambda shape: pl.BlockSpec(shape, lambda i: (0, 0), pipeline_mode=pl.Buffered(1))
    rows = lambda w: pl.BlockSpec((tm, w), lambda i: (i, 0))
    return pl.pallas_call(
        _outproj_kernel,
        out_shape=(jax.ShapeDtypeStruct((T, D), f32), jax.ShapeDtypeStruct((T, D), h2_dtype)),
        grid=(T // tm,),
        in_specs=[rows(GDN_WIDTH), rows(SSM_WIDTH), rows(NA_WIDTH), const((1, SSM_WIDTH)),
                  const((GDN_WIDTH, D)), const((SSM_WIDTH, D)), const((NA_WIDTH, D)),
                  rows(D), pl.BlockSpec((1, 6, D), lambda i: (i // per_b, 0, 0)), const((1, D))],
        out_specs=(rows(D), rows(D)),
        scratch_shapes=[pltpu.VMEM((tm, SSM_WIDTH), bf16)],
        compiler_params=_cparams(("arbitrary",), 48),
        name="outproj",
    )(ya, yb, yc, ssm_norm, wa, wb, wc, x2, mod_l, nw)


def _swiglu_acc(h_ref, w1_ref, w3_ref, w2_ref, acc_ref, first):
    h = h_ref[...]
    a = jnp.dot(h, w1_ref[...], preferred_element_type=f32)
    b = jnp.dot(h, w3_ref[...], preferred_element_type=f32)
    u = (a * _sigmoid(a) * b).astype(bf16)
    part = jnp.dot(u, w2_ref[...], preferred_element_type=f32)

    @pl.when(first)
    def _():
        acc_ref[...] = part

    @pl.when(jnp.logical_not(first))
    def _():
        acc_ref[...] += part


def _ffn_kernel(h_ref, w1_ref, w3_ref, w2_ref, x_ref, mod_ref, o_ref, acc_ref):
    f = pl.program_id(1)
    _swiglu_acc(h_ref, w1_ref, w3_ref, w2_ref, acc_ref, f == 0)

    @pl.when(f == pl.num_programs(1) - 1)
    def _():
        o_ref[...] = x_ref[...] + mod_ref[0, 5:6, :] * acc_ref[...]


def _ffn_dense(h2, w1, w3, w2, x2, mod_l, L):
    T, D = x2.shape
    F = w1.shape[1]
    tm, tf = 512, 512
    per_b = L // tm
    return pl.pallas_call(
        _ffn_kernel,
        out_shape=jax.ShapeDtypeStruct((T, D), f32),
        grid=(T // tm, F // tf),
        in_specs=[pl.BlockSpec((tm, D), lambda i, f: (i, 0)),
                  pl.BlockSpec((D, tf), lambda i, f: (0, f)),
                  pl.BlockSpec((D, tf), lambda i, f: (0, f)),
                  pl.BlockSpec((tf, D), lambda i, f: (f, 0)),
                  pl.BlockSpec((tm, D), lambda i, f: (i, 0)),
                  pl.BlockSpec((1, 6, D), lambda i, f: (i // per_b, 0, 0))],
        out_specs=pl.BlockSpec((tm, D), lambda i, f: (i, 0)),
        scratch_shapes=[pltpu.VMEM((tm, D), f32)],
        compiler_params=_cparams(("arbitrary", "arbitrary"), 48),
        name="ffn_dense",
    )(h2, w1, w3, w2, x2, mod_l)


MOE_TM = 512


def _moe_ffn_kernel(te_ref, nu_ref, h_ref, w1_ref, w3_ref, w2_ref, o_ref, acc_ref):
    i = pl.program_id(0)
    f = pl.program_id(1)
    last = f == pl.num_programs(1) - 1
    used = i < nu_ref[0]

    @pl.when(used)
    def _():
        _swiglu_acc(h_ref, w1_ref, w3_ref, w2_ref, acc_ref, f == 0)

    @pl.when(jnp.logical_and(used, last))
    def _():
        o_ref[...] = acc_ref[...]

    @pl.when(jnp.logical_and(jnp.logical_not(used), last))
    def _():
        o_ref[...] = jnp.zeros(o_ref.shape, f32)


def _moe_ffn(hs, w1, w3, w2, tile_expert, n_used):
    P, D = hs.shape
    F = w1.shape[2]
    tm, tf = MOE_TM, 512
    nf = F // tf

    def fe(i, f, nu):
        return jnp.where(i < nu[0], f, nf - 1)

    grid_spec = pltpu.PrefetchScalarGridSpec(
        num_scalar_prefetch=2,
        grid=(P // tm, nf),
        in_specs=[pl.BlockSpec((tm, D), lambda i, f, te, nu: (jnp.minimum(i, nu[0] - 1), 0)),
                  pl.BlockSpec((1, D, tf), lambda i, f, te, nu: (te[i], 0, fe(i, f, nu))),
                  pl.BlockSpec((1, D, tf), lambda i, f, te, nu: (te[i], 0, fe(i, f, nu))),
                  pl.BlockSpec((1, tf, D), lambda i, f, te, nu: (te[i], fe(i, f, nu), 0))],
        out_specs=pl.BlockSpec((tm, D), lambda i, f, te, nu: (i, 0)),
        scratch_shapes=[pltpu.VMEM((tm, D), f32)])
    kern = lambda te, nu, h, a, b, c, o, acc: _moe_ffn_kernel(
        te, nu, h, a.at[0], b.at[0], c.at[0], o, acc)
    return pl.pallas_call(
        kern,
        out_shape=jax.ShapeDtypeStruct((P, D), f32),
        grid_spec=grid_spec,
        compiler_params=_cparams(("arbitrary", "arbitrary"), 48),
        name="moe_ffn",
    )(tile_expert, n_used, hs, w1, w3, w2)


def _router_kernel(h_ref, wr_ref, o_ref):
    h = h_ref[...]
    hh, hm, _ = _split3(h)
    w = wr_ref[...]
    wh, wm, _ = _split3(w)
    logits = (jnp.dot(hh, wh, preferred_element_type=f32) + jnp.dot(hh, wm, preferred_element_type=f32)
              + jnp.dot(hm, wh, preferred_element_type=f32))
    lane = lax.broadcasted_iota(i32, logits.shape, 1)
    logits = jnp.where(lane < N_EXPERTS, logits, -jnp.inf)
    m1 = jnp.max(logits, axis=-1, keepdims=True)
    i1 = jnp.min(jnp.where(logits == m1, lane, LANE), axis=-1, keepdims=True)
    rest = jnp.where(lane == i1, -jnp.inf, logits)
    m2 = jnp.max(rest, axis=-1, keepdims=True)
    i2 = jnp.min(jnp.where(rest == m2, lane, LANE), axis=-1, keepdims=True)
    e2 = jnp.exp(m2 - m1)
    g1 = 1.0 / (1.0 + e2)
    g2 = e2 / (1.0 + e2)
    o_ref[...] = jnp.where(lane == 0, g1, jnp.where(lane == 1, g2, jnp.where(
        lane == 2, i1.astype(f32), jnp.where(lane == 3, i2.astype(f32), 0.0))))


def _router(h2, wr_pad):
    T, D = h2.shape
    tm = 512
    return pl.pallas_call(
        _router_kernel,
        out_shape=jax.ShapeDtypeStruct((T, LANE), f32),
        grid=(T // tm,),
        in_specs=[pl.BlockSpec((tm, D), lambda i: (i, 0)), pl.BlockSpec((D, LANE), lambda i: (0, 0))],
        out_specs=pl.BlockSpec((tm, LANE), lambda i: (i, 0)),
        compiler_params=_cparams(("arbitrary",), 32),
        name="moe_router",
    )(h2, wr_pad)


GATHER_ROWS = 256


def _row_copy(src_hbm, buf, sem, t, r):
    return pltpu.make_async_copy(src_hbm.at[pl.ds(t, 1), :], buf.at[pl.ds(r, 1), :], sem)


def _gather_rows_into(idx_ref, src_hbm, buf, sem):
    n = buf.shape[0]

    def issue(r, carry):
        _row_copy(src_hbm, buf, sem, idx_ref[0, 0, r], r).start()
        return carry

    lax.fori_loop(0, n, issue, 0)

    def wait(r, carry):
        _row_copy(src_hbm, buf, sem, 0, r).wait()
        return carry

    lax.fori_loop(0, n, wait, 0)


def _gather_kernel(idx_ref, src_hbm, o_ref, buf, sem):
    _gather_rows_into(idx_ref, src_hbm, buf, sem)
    o_ref[...] = buf[...].astype(o_ref.dtype)


def _gather_rows(src, idx, out_dtype):
    P = idx.shape[0]
    D = src.shape[1]
    rt = GATHER_ROWS
    return pl.pallas_call(
        _gather_kernel,
        out_shape=jax.ShapeDtypeStruct((P, D), out_dtype),
        grid=(P // rt,),
        in_specs=[pl.BlockSpec((1, 1, rt), lambda i: (i, 0, 0), memory_space=pltpu.SMEM),
                  pl.BlockSpec(memory_space=pl.ANY)],
        out_specs=pl.BlockSpec((rt, D), lambda i: (i, 0)),
        scratch_shapes=[pltpu.VMEM((rt, D), f32), pltpu.SemaphoreType.DMA(())],
        compiler_params=_cparams(("arbitrary",), 32),
        name="moe_gather",
    )(idx.reshape(P // rt, 1, rt), src)


def _combine_kernel(p0_ref, p1_ref, ys_hbm, x_ref, gt_ref, mod_ref, nf_ref, o_ref, b0, b1, sem0, sem1,
                    *, final_norm):
    _gather_rows_into(p0_ref, ys_hbm, b0, sem0)
    _gather_rows_into(p1_ref, ys_hbm, b1, sem1)
    gt = gt_ref[...]
    y = gt[:, 0:1] * b0[...] + gt[:, 1:2] * b1[...]
    xn = x_ref[...] + mod_ref[0, 5:6, :] * y
    if final_norm:
        xn = _rms(xn) * nf_ref[...]
    o_ref[...] = xn


def _moe_combine(ys, pos0, pos1, x2, gates, mod_l, norm_final, L, final_norm):
    T, D = x2.shape
    rt = GATHER_ROWS
    per_b = L // rt
    idx = lambda: pl.BlockSpec((1, 1, rt), lambda i: (i, 0, 0), memory_space=pltpu.SMEM)
    return pl.pallas_call(
        functools.partial(_combine_kernel, final_norm=final_norm),
        out_shape=jax.ShapeDtypeStruct((T, D), f32),
        grid=(T // rt,),
        in_specs=[idx(), idx(), pl.BlockSpec(memory_space=pl.ANY),
                  pl.BlockSpec((rt, D), lambda i: (i, 0)),
                  pl.BlockSpec((rt, LANE), lambda i: (i, 0)),
                  pl.BlockSpec((1, 6, D), lambda i: (i // per_b, 0, 0)),
                  pl.BlockSpec((1, D), lambda i: (0, 0))],
        out_specs=pl.BlockSpec((rt, D), lambda i: (i, 0)),
        scratch_shapes=[pltpu.VMEM((rt, D), f32), pltpu.VMEM((rt, D), f32),
                        pltpu.SemaphoreType.DMA(()), pltpu.SemaphoreType.DMA(())],
        compiler_params=_cparams(("arbitrary",), 32),
        name="moe_combine",
    )(pos0.reshape(T // rt, 1, rt), pos1.reshape(T // rt, 1, rt), ys, x2, gates, mod_l, norm_final)


def _moe_plan(route, T):
    tm = MOE_TM
    P = 2 * T + N_EXPERTS * tm
    e_flat = route[:, 2:4].astype(i32).reshape(-1)
    onehot = (e_flat[:, None] == jnp.arange(N_EXPERTS, dtype=i32)[None, :]).astype(i32)
    csum = jnp.cumsum(onehot, axis=0)
    counts = csum[-1]
    rank = jnp.sum(csum * onehot, axis=1) - 1
    tiles = (counts + tm - 1) // tm
    tile_end = jnp.cumsum(tiles)
    off = (tile_end - tiles) * tm
    pos = off[e_flat] + rank
    tok = jnp.arange(2 * T, dtype=i32) // 2
    src = jnp.zeros((P,), i32).at[pos].set(tok, unique_indices=True)
    n_used = tile_end[-1:]
    tidx = jnp.arange(P // tm, dtype=i32)
    te = jnp.sum((tidx[:, None] >= tile_end[None, :]).astype(i32), axis=1)
    te_last = jnp.sum((n_used - 1 >= tile_end).astype(i32))
    te = jnp.where(tidx < n_used, jnp.minimum(te, N_EXPERTS - 1), te_last).astype(i32)
    pos2 = pos.reshape(T, 2)
    return src, pos2[:, 0], pos2[:, 1], te, n_used.astype(i32)


def _permute_w_in(w_in):
    depth, d, _ = w_in.shape
    pad = jnp.zeros((depth, d, PROJ_W - OFF_SM - 48), w_in.dtype)
    parts = [w_in[:, :, 0:3072], w_in[:, :, 4376:5144], w_in[:, :, 3584:4352], w_in[:, :, 3072:3584],
             w_in[:, :, 5168:6704], w_in[:, :, 4352:4376], w_in[:, :, 5144:5168], pad]
    return jnp.concatenate(parts, axis=-1).astype(bf16)


def _small_rows(gdn_p, ssm_p):
    depth = gdn_p.shape[0]
    row = jnp.zeros((depth, LANE), f32)
    row = row.at[:, SM_GA:SM_GA + 2 * GDN_HEADS].set(gdn_p.reshape(depth, -1).astype(f32))
    row = row.at[:, SM_DT:SM_DT + 2 * SSM_HEADS].set(ssm_p.reshape(depth, -1).astype(f32))
    return row.reshape(depth, 1, LANE)


def kernel(x, c, ada_w, ada_b, norm_mix, norm_ffn, norm_final, w_in, conv_w, conv_b, gdn_A_log, gdn_dt_bias, gdn_norm, ssm_A_log, ssm_dt_bias, ssm_D, ssm_norm, na_rpb, w_out, ffn_w1, ffn_w3, ffn_w2, moe_router, moe_w1, moe_w3, moe_w2):
    bsz, L, D = x.shape
    depth = w_in.shape[0]
    T = bsz * L

    mod = _ada_mod(c, ada_w, ada_b).reshape(depth, bsz, 6, D)
    w_in_p = _permute_w_in(w_in)
    w_out_b = w_out.astype(bf16)
    alog_rows = _small_rows(gdn_A_log, ssm_A_log)
    dtb_rows = _small_rows(gdn_dt_bias, ssm_dt_bias)
    dskip_rows = jnp.repeat(ssm_D.astype(f32), SSM_HEADDIM, axis=-1).reshape(depth, 1, SSM_WIDTH)
    ffn_w1b, ffn_w3b, ffn_w2b = ffn_w1.astype(bf16), ffn_w3.astype(bf16), ffn_w2.astype(bf16)
    moe_w1b, moe_w3b, moe_w2b = moe_w1.astype(bf16), moe_w3.astype(bf16), moe_w2.astype(bf16)
    wr_pad = jnp.pad(moe_router.astype(f32), ((0, 0), (0, 0), (0, LANE - N_EXPERTS)))

    x2 = x.reshape(T, D)
    for l in range(depth):
        is_moe = l % 2 == 1
        proj = _inproj(x2, mod[l], norm_mix[l].reshape(1, D), w_in_p[l], L).reshape(bsz, L, PROJ_W)
        cb = conv_b[l].reshape(1, CONV_CH)
        ya = _gdn_mixer(proj, conv_w[l], cb, alog_rows[l], dtb_rows[l], gdn_norm[l].reshape(1, LANE))
        yb = _ssd_mixer(proj, conv_w[l], cb, alog_rows[l], dtb_rows[l], dskip_rows[l])
        yc = _na_mixer(proj, _na_bias_table(na_rpb[l]))
        x2, h2 = _outproj(ya.reshape(T, GDN_WIDTH), yb.reshape(T, SSM_WIDTH), yc.reshape(T, NA_WIDTH),
                          ssm_norm[l].reshape(1, SSM_WIDTH), w_out_b[l, :GDN_WIDTH],
                          w_out_b[l, GDN_WIDTH:GDN_WIDTH + SSM_WIDTH], w_out_b[l, GDN_WIDTH + SSM_WIDTH:],
                          x2, mod[l], norm_ffn[l].reshape(1, D), L, f32 if is_moe else bf16)
        if not is_moe:
            x2 = _ffn_dense(h2, ffn_w1b[l // 2], ffn_w3b[l // 2], ffn_w2b[l // 2], x2, mod[l], L)
        else:
            m = l // 2
            route = _router(h2, wr_pad[m])
            src, pos0, pos1, te, n_used = _moe_plan(route, T)
            hs = _gather_rows(h2, src, bf16)
            ys = _moe_ffn(hs, moe_w1b[m], moe_w3b[m], moe_w2b[m], te, n_used)
            x2 = _moe_combine(ys, pos0, pos1, x2, route, mod[l], norm_final.reshape(1, D), L,
                              final_norm=(l == depth - 1))
    if depth % 2 == 1:
        raise NotImplementedError("the final norm is fused into the last MoE combine")
    return x2.reshape(bsz, L, D)
```

```python
import functools
import math

import jax
import jax.numpy as jnp
import numpy as np
from jax import lax
from jax.experimental import pallas as pl
from jax.experimental.pallas import tpu as pltpu

f32 = jnp.float32
bf16 = jnp.bfloat16
i32 = jnp.int32

GRID_W = 64
GDN_HEADS = 6
GDN_DK = 128
GDN_WIDTH = 768
SSM_HEADS = 12
SSM_HEADDIM = 64
SSM_WIDTH = 768
SSM_GROUPS = 2
SSM_HPG = 6
SSM_STATE = 128
NA_HEADS = 8
NA_HEADDIM = 64
NA_WIDTH = 512
NA_KH = 8
NA_KW = 16
CONV_K = 5
CONV_CH = 3584
N_EXPERTS = 8
EPS = 1e-6

OFF_GQ, OFF_GK, OFF_GV, OFF_SX, OFF_SZ, OFF_GZ = 0, 768, 1536, 2304, 3072, 3840
OFF_SB, OFF_SC, OFF_NQ, OFF_NK, OFF_NV, OFF_SM = 4608, 4864, 5120, 5632, 6144, 6656
PROJ_W = 6912
SM_BETA, SM_GA, SM_DT = 0, 12, 24

LANE = 128
MIX_CHUNK = 128
CONV_HALO = 8
MIB = 1024 * 1024


def _cparams(sem, vmem_mib):
    return pltpu.CompilerParams(dimension_semantics=sem, vmem_limit_bytes=vmem_mib * MIB)


def _sigmoid(x):
    return 1.0 / (1.0 + jnp.exp(-x))


def _softplus(x):
    return jnp.maximum(x, 0.0) + jnp.log1p(jnp.exp(-jnp.abs(x)))


def _dot(a, b):
    return jnp.dot(a.astype(bf16), b.astype(bf16), preferred_element_type=f32)


def _dot_nt(a, b):
    return lax.dot_general(a.astype(bf16), b.astype(bf16), (((1,), (1,)), ((), ())),
                           preferred_element_type=f32)


def _split(x, terms):
    out = []
    for _ in range(terms - 1):
        piece = x.astype(bf16)
        out.append(piece)
        x = x - piece.astype(f32)
    out.append(x.astype(bf16))
    return out


def _sel_left(m01, x, terms=3):
    m = m01.astype(bf16)
    return sum(jnp.dot(m, t, preferred_element_type=f32) for t in _split(x, terms))


def _sel_right(x, m01, terms=2):
    m = m01.astype(bf16)
    return sum(jnp.dot(t, m, preferred_element_type=f32) for t in _split(x, terms))


def _rms(x):
    return x * lax.rsqrt(jnp.mean(x * x, axis=-1, keepdims=True) + EPS)


def _conv_silu(xp_ref, cw_ref, cb_ref, r0, rows, blk=0):
    lanes = slice(blk * LANE, (blk + 1) * LANE)
    acc = jnp.broadcast_to(cb_ref[:, lanes], (rows, LANE))
    for j in range(CONV_K):
        shift = CONV_HALO - CONV_K // 2 + j
        acc = acc + cw_ref[j:j + 1, lanes] * xp_ref[blk, pl.ds(r0 + shift, rows), :]
    return acc * _sigmoid(acc)


def _fill_padded(src_ref, dst_ref, L):
    for blk in range(dst_ref.shape[0]):
        dst_ref[blk, 0:CONV_HALO, :] = jnp.zeros((CONV_HALO, LANE), f32)
        dst_ref[blk, L + CONV_HALO:L + 2 * CONV_HALO, :] = jnp.zeros((CONV_HALO, LANE), f32)
        dst_ref[blk, CONV_HALO:L + CONV_HALO, :] = src_ref[0, :, blk * LANE:(blk + 1) * LANE]


def _pick_col(blk, idx):
    lane = lax.broadcasted_iota(i32, blk.shape, 1)
    return jnp.sum(jnp.where(lane == idx, blk, 0.0), axis=-1, keepdims=True)


def _ada_kernel(c_ref, w_ref, b_ref, o_ref):
    c = c_ref[...]
    cond = c * _sigmoid(c)
    o_ref[0] = _dot(cond, w_ref[0]) + b_ref[0]


def _ada_mod(c, ada_w, ada_b):
    depth, d, n = ada_w.shape
    bsz = c.shape[0]
    tn = 1536
    return pl.pallas_call(
        _ada_kernel,
        out_shape=jax.ShapeDtypeStruct((depth, bsz, n), f32),
        grid=(depth, n // tn),
        in_specs=[pl.BlockSpec((bsz, d), lambda l, j: (0, 0)),
                  pl.BlockSpec((1, d, tn), lambda l, j: (l, 0, j)),
                  pl.BlockSpec((1, 1, tn), lambda l, j: (l, 0, j))],
        out_specs=pl.BlockSpec((1, bsz, tn), lambda l, j: (l, 0, j)),
        compiler_params=_cparams(("arbitrary", "arbitrary"), 40),
        name="ada_mod",
    )(c, ada_w, ada_b.reshape(depth, 1, n))


NORM_ROWS = 128


def _norm_mod_rows(x_ref, nw, sc, sh, h_ref):
    def body(i, carry):
        r = pl.multiple_of(i * NORM_ROWS, NORM_ROWS)
        y = _rms(x_ref[pl.ds(r, NORM_ROWS), :]) * nw
        h_ref[pl.ds(r, NORM_ROWS), :] = (y * (1.0 + sc) + sh).astype(h_ref.dtype)
        return carry
    lax.fori_loop(0, x_ref.shape[0] // NORM_ROWS, body, 0)


def _inproj_kernel(x_ref, mod_ref, nw_ref, w_ref, o_ref, h_ref):
    @pl.when(pl.program_id(1) == 0)
    def _():
        _norm_mod_rows(x_ref, nw_ref[...], mod_ref[0, 1:2, :], mod_ref[0, 0:1, :], h_ref)
    o_ref[...] = jnp.dot(h_ref[...], w_ref[...], preferred_element_type=f32)


def _mod_spec(l, per_b, nargs):
    if nargs == 1:
        return lambda i: (l, i // per_b, 0, 0)
    return lambda i, j: (l, i // per_b, 0, 0)


def _inproj(x2, mod, nw, w_all, l, L):
    T, D = x2.shape
    NP = w_all.shape[2]
    tm, tn = min(1024, L), 768
    per_b = L // tm
    return pl.pallas_call(
        _inproj_kernel,
        out_shape=jax.ShapeDtypeStruct((T, NP), f32),
        grid=(T // tm, NP // tn),
        in_specs=[pl.BlockSpec((tm, D), lambda i, j: (i, 0)),
                  pl.BlockSpec((None, 1, 6, D), _mod_spec(l, per_b, 2)),
                  pl.BlockSpec((1, D), lambda i, j: (0, 0)),
                  pl.BlockSpec((None, D, tn), lambda i, j: (l, 0, j))],
        out_specs=pl.BlockSpec((tm, tn), lambda i, j: (i, j)),
        scratch_shapes=[pltpu.VMEM((tm, D), bf16)],
        compiler_params=_cparams(("arbitrary", "arbitrary"), 44),
        name="inproj",
    )(x2, mod, nw, w_all)


def _prep_kernel(sm_ref, alog_ref, dtb_ref, act_ref, cum_ref, cumt_ref, tot_ref):
    L = sm_ref.shape[1]
    C = MIX_CHUNK
    ri = lax.broadcasted_iota(i32, (C, C), 0)
    ci = lax.broadcasted_iota(i32, (C, C), 1)
    trilf = jnp.where(ri >= ci, 1.0, 0.0)
    lane = lax.broadcasted_iota(i32, (C, LANE), 1)
    gdn_rev = jnp.logical_and(lane >= SM_GA + GDN_HEADS, lane < SM_DT)
    backward = jnp.logical_or(gdn_rev, lane >= SM_DT + SSM_HEADS)
    neg_a = -jnp.exp(alog_ref[...])
    dtb = dtb_ref[...]

    def body(c, carry):
        r0 = pl.multiple_of(c * C, C)
        sm = sm_ref[0, pl.ds(r0, C), :]
        sp = _softplus(sm + dtb)
        act_ref[0, pl.ds(r0, C), :] = jnp.where(lane < SM_GA, _sigmoid(sm), sp)
        a = neg_a * sp
        cum = _sel_left(trilf, a)
        tot = cum[C - 1:C, :]
        cd = jnp.where(backward, tot - cum + a, cum)
        cum_ref[0, pl.ds(r0, C), :] = cd
        cumt_ref[0, c] = cd.T
        tot_ref[0, c] = jnp.broadcast_to(tot, (8, LANE))
        return carry

    lax.fori_loop(0, L // C, body, 0)


def _mix_prep(proj, alog_row, dtb_row):
    bsz, L, _ = proj.shape
    C = MIX_CHUNK
    nc = L // C
    row = pl.BlockSpec((1, LANE), lambda b: (0, 0))
    tok = pl.BlockSpec((1, L, LANE), lambda b: (b, 0, 0))
    return pl.pallas_call(
        _prep_kernel,
        out_shape=(jax.ShapeDtypeStruct((bsz, L, LANE), f32), jax.ShapeDtypeStruct((bsz, L, LANE), f32),
                   jax.ShapeDtypeStruct((bsz, nc, LANE, C), f32), jax.ShapeDtypeStruct((bsz, nc, 8, LANE), f32)),
        grid=(bsz,),
        in_specs=[pl.BlockSpec((1, L, LANE), lambda b: (b, 0, OFF_SM // LANE)), row, row],
        out_specs=(tok, tok, pl.BlockSpec((1, nc, LANE, C), lambda b: (b, 0, 0, 0)),
                   pl.BlockSpec((1, nc, 8, LANE), lambda b: (b, 0, 0, 0))),
        compiler_params=_cparams(("arbitrary",), 32),
        name="mix_prep",
    )(proj, alog_row, dtb_row)


GDN_GROUP = 4
INV_BASE = 16


def _unit_tri_inverses(a_mats):
    C = a_mats[0].shape[0]
    ri = lax.broadcasted_iota(i32, (C, C), 0)
    ci = lax.broadcasted_iota(i32, (C, C), 1)
    eye = jnp.where(ri == ci, 1.0, 0.0)

    def same(b):
        s = int(math.log2(b))
        return lax.shift_right_logical(ri, s) == lax.shift_right_logical(ci, s)

    ps = [jnp.where(same(INV_BASE), a, 0.0) for a in a_mats]
    ts = [eye - p for p in ps]
    for _ in range(int(math.log2(INV_BASE)) - 1):
        ps = [_dot(p, p) for p in ps]
        ts = [t + _dot(t, p) for t, p in zip(ts, ps)]
    b = INV_BASE
    while b < C:
        off = jnp.logical_and(same(2 * b), jnp.logical_not(same(b)))
        us = [_dot(jnp.where(off, a, 0.0), t) for a, t in zip(a_mats, ts)]
        ts = [t - _dot(t, u) for t, u in zip(ts, us)]
        b *= 2
    return ts


def _gdn_kernel(q_ref, k_ref, v_ref, z_ref, act_ref, cum_ref, cumt_ref, tot_ref,
                cwq, cwk, cwv, cbq, cbk, cbv, nw_ref, o_ref,
                qp, kp, vp, kmat_s, nmat_s, qeff_s, gl_s, oacc):
    h = pl.program_id(1)
    L = q_ref.shape[1]
    C = MIX_CHUNK
    nc = L // C
    _fill_padded(q_ref, qp, L)
    _fill_padded(k_ref, kp, L)
    _fill_padded(v_ref, vp, L)
    oacc[...] = jnp.zeros(oacc.shape, f32)

    def phase_a(gi, carry):
        ri = lax.broadcasted_iota(i32, (C, C), 0)
        ci = lax.broadcasted_iota(i32, (C, C), 1)
        chains = []
        for j in range(GDN_GROUP):
            c = gi * GDN_GROUP + j
            r0 = pl.multiple_of(c * C, C)
            q = _conv_silu(qp, cwq, cbq, r0, C)
            k = _conv_silu(kp, cwk, cbk, r0, C)
            v = _conv_silu(vp, cwv, cbv, r0, C)
            qn = q * lax.rsqrt(jnp.sum(q * q, axis=-1, keepdims=True) + EPS) * (GDN_DK ** -0.5)
            kn = k * lax.rsqrt(jnp.sum(k * k, axis=-1, keepdims=True) + EPS)
            kk = _dot_nt(kn, kn)
            qk = _dot_nt(qn, kn)
            act = act_ref[0, pl.ds(r0, C), :]
            cum = cum_ref[0, pl.ds(r0, C), :]
            for d in (0, 1):
                gidx = SM_GA + d * GDN_HEADS + h
                beta = _pick_col(act, SM_BETA + d * GDN_HEADS + h)
                gcol = _pick_col(cum, gidx)
                grow = cumt_ref[0, c, pl.ds(gidx, 1), :]
                gl = _pick_col(tot_ref[0, c, 0:1, :], gidx)
                mask = (ri >= ci) if d == 0 else (ri <= ci)
                strict = (ri > ci) if d == 0 else (ri < ci)
                dec = jnp.exp(jnp.where(mask, gcol - grow, -jnp.inf))
                a_mat = jnp.where(strict, beta * kk * dec, 0.0)
                chains.append((c, r0, d, a_mat, dec, beta, gcol, gl, qn, kn, v, qk))
        t_invs = _unit_tri_inverses([ch[3] for ch in chains])
        for (c, r0, d, _, dec, beta, gcol, gl, qn, kn, v, qk), t_inv in zip(chains, t_invs):
            eg = jnp.exp(gcol)
            sol = _dot(t_inv, jnp.concatenate([v * beta, kn * (beta * eg)], axis=1))
            kd = kn * jnp.exp(gl - gcol)
            big = _dot(jnp.concatenate([kd.T, qk * dec], axis=0), sol)
            nmat_s[d, c] = big[0:GDN_DK, 0:LANE]
            kmat_s[d, c] = big[0:GDN_DK, LANE:].astype(bf16)
            qeff_s[d, pl.ds(r0, C), :] = (qn * eg - big[GDN_DK:, LANE:]).astype(bf16)
            oacc[pl.ds(r0, C), :] += big[GDN_DK:, 0:LANE]
            gl_s[d, c] = jnp.broadcast_to(jnp.exp(gl), (8, LANE))
        return carry

    lax.fori_loop(0, nc // GDN_GROUP, phase_a, 0)

    def phase_b(i, carry):
        out = []
        for d in (0, 1):
            s = carry[d]
            c = i if d == 0 else nc - 1 - i
            r0 = pl.multiple_of(c * C, C)
            sb = s.astype(bf16)
            oacc[pl.ds(r0, C), :] += jnp.dot(qeff_s[d, pl.ds(r0, C), :], sb, preferred_element_type=f32)
            out.append(s * gl_s[d, c][0:1, :] - jnp.dot(kmat_s[d, c], sb, preferred_element_type=f32)
                       + nmat_s[d, c])
        return tuple(out)

    s0 = jnp.zeros((GDN_DK, LANE), f32)
    lax.fori_loop(0, nc, phase_b, (s0, s0))

    def fin(c, carry):
        r0 = pl.multiple_of(c * C, C)
        y = _rms(oacc[pl.ds(r0, C), :]) * nw_ref[...]
        z = z_ref[0, pl.ds(r0, C), :]
        o_ref[0, pl.ds(r0, C), :] = (y * (z * _sigmoid(z))).astype(o_ref.dtype)
        return carry

    lax.fori_loop(0, nc, fin, 0)


def _prep_specs(L, nc, C):
    tok = pl.BlockSpec((1, L, LANE), lambda b, h: (b, 0, 0))
    return [tok, tok, pl.BlockSpec((1, nc, LANE, C), lambda b, h: (b, 0, 0, 0)),
            pl.BlockSpec((1, nc, 8, LANE), lambda b, h: (b, 0, 0, 0))]


def _gdn_mixer(proj, prep, conv_w, conv_b, gdn_norm):
    bsz, L, _ = proj.shape
    C = MIX_CHUNK
    nc = L // C
    assert nc % GDN_GROUP == 0
    blk = lambda off: pl.BlockSpec((1, L, LANE), lambda b, h, o=off // LANE: (b, 0, o + h))
    cwb = lambda off: pl.BlockSpec((CONV_K, LANE), lambda b, h, o=off // LANE: (0, o + h))
    cbb = lambda off: pl.BlockSpec((1, LANE), lambda b, h, o=off // LANE: (0, o + h))
    return pl.pallas_call(
        _gdn_kernel,
        out_shape=jax.ShapeDtypeStruct((bsz, L, GDN_WIDTH), bf16),
        grid=(bsz, GDN_HEADS),
        in_specs=[blk(OFF_GQ), blk(OFF_GK), blk(OFF_GV), blk(OFF_GZ)] + _prep_specs(L, nc, C) + [
            cwb(0), cwb(768), cwb(1536), cbb(0), cbb(768), cbb(1536),
            pl.BlockSpec((1, LANE), lambda b, h: (0, 0))],
        out_specs=pl.BlockSpec((1, L, LANE), lambda b, h: (b, 0, h)),
        scratch_shapes=[pltpu.VMEM((1, L + 2 * CONV_HALO, LANE), f32)] * 3 + [
            pltpu.VMEM((2, nc, GDN_DK, LANE), bf16),
            pltpu.VMEM((2, nc, GDN_DK, LANE), f32),
            pltpu.VMEM((2, L, LANE), bf16),
            pltpu.VMEM((2, nc, 8, LANE), f32),
            pltpu.VMEM((L, LANE), f32)],
        compiler_params=_cparams(("arbitrary", "arbitrary"), 40),
        name="gdn_mixer",
    )(proj, proj, proj, proj, *prep, conv_w, conv_w, conv_w, conv_b, conv_b, conv_b, gdn_norm)


SSD_W = SSM_HPG * SSM_HEADDIM


def _ssd_kernel(x_ref, z_ref, b_ref, c_ref, act_ref, cum_ref, cumt_ref, tot_ref,
                cwx, cwb, cwc, cbx, cbb, cbc, dsk_ref, o_ref, xp, bp, cp, xc, bc, cc, yacc):
    g = pl.program_id(1)
    L = x_ref.shape[1]
    C = MIX_CHUNK
    nc = L // C
    _fill_padded(x_ref, xp, L)
    _fill_padded(b_ref, bp, L)
    _fill_padded(c_ref, cp, L)
    yacc[...] = jnp.zeros(yacc.shape, f32)

    def conv_all(c, carry):
        r0 = pl.multiple_of(c * C, C)
        for blk in range(SSD_W // LANE):
            xc[pl.ds(r0, C), blk * LANE:(blk + 1) * LANE] = _conv_silu(xp, cwx, cbx, r0, C, blk)
        bc[pl.ds(r0, C), :] = _conv_silu(bp, cwb, cbb, r0, C)
        cc[pl.ds(r0, C), :] = _conv_silu(cp, cwc, cbc, r0, C)
        return carry

    lax.fori_loop(0, nc, conv_all, 0)

    def step(i, carry):
        ri = lax.broadcasted_iota(i32, (C, C), 0)
        ci = lax.broadcasted_iota(i32, (C, C), 1)
        er = lax.broadcasted_iota(i32, (LANE, SSD_W), 0)
        el = lax.shift_right_logical(lax.broadcasted_iota(i32, (LANE, SSD_W), 1),
                                     int(math.log2(SSM_HEADDIM)))
        half = lax.broadcasted_iota(i32, (C, LANE), 1) < SSM_HEADDIM
        out = []
        for d in (0, 1):
            ht = carry[d]
            c = i if d == 0 else nc - 1 - i
            r0 = pl.multiple_of(c * C, C)
            base = SM_DT + d * SSM_HEADS + g * SSM_HPG
            expand = jnp.where(er == base + el, 1.0, 0.0)
            cum = cum_ref[0, pl.ds(r0, C), :]
            g6 = _sel_right(cum, expand)
            dt6 = _sel_right(act_ref[0, pl.ds(r0, C), :], expand)
            tot6 = _sel_right(tot_ref[0, c], expand)[0:1, :]
            xdt = xc[pl.ds(r0, C), :] * dt6
            bm = bc[pl.ds(r0, C), :]
            cm = cc[pl.ds(r0, C), :]
            cb = _dot_nt(cm, bm)
            mask = (ri >= ci) if d == 0 else (ri <= ci)
            pieces = []
            for pr in range(SSM_HPG // 2):
                xpair = xdt[:, pr * LANE:(pr + 1) * LANE].astype(bf16)
                ys = []
                for e in (0, 1):
                    gidx = base + 2 * pr + e
                    diff = _pick_col(cum, gidx) - cumt_ref[0, c, pl.ds(gidx, 1), :]
                    lm = jnp.exp(jnp.where(mask, diff, -jnp.inf))
                    ys.append(jnp.dot((cb * lm).astype(bf16), xpair, preferred_element_type=f32))
                pieces.append(jnp.where(half, ys[0], ys[1]))
            y_diag = jnp.concatenate(pieces, axis=1)
            y_off = _dot(cm, ht) * jnp.exp(g6)
            yacc[pl.ds(r0, C), :] += y_diag + y_off
            st = _dot(bm.T, xdt * jnp.exp(tot6 - g6))
            out.append(ht * jnp.exp(tot6) + st)
        return tuple(out)

    h0 = jnp.zeros((SSM_STATE, SSD_W), f32)
    lax.fori_loop(0, nc, step, (h0, h0))

    def fin(c, carry):
        r0 = pl.multiple_of(c * C, C)
        y = yacc[pl.ds(r0, C), :] + xc[pl.ds(r0, C), :] * dsk_ref[...]
        z = z_ref[0, pl.ds(r0, C), :]
        o_ref[0, pl.ds(r0, C), :] = y * (z * _sigmoid(z))
        return carry

    lax.fori_loop(0, nc, fin, 0)


def _ssd_mixer(proj, prep, conv_w, conv_b, dskip_row):
    bsz, L, _ = proj.shape
    C = MIX_CHUNK
    nc = L // C
    wide = lambda off: pl.BlockSpec((1, L, SSD_W), lambda b, g, o=off // SSD_W: (b, 0, o + g))
    nar = lambda off: pl.BlockSpec((1, L, LANE), lambda b, g, o=off // LANE: (b, 0, o + g))
    return pl.pallas_call(
        _ssd_kernel,
        out_shape=jax.ShapeDtypeStruct((bsz, L, SSM_WIDTH), f32),
        grid=(bsz, SSM_GROUPS),
        in_specs=[wide(OFF_SX), wide(OFF_SZ), nar(OFF_SB), nar(OFF_SC)] + _prep_specs(L, nc, C) + [
            pl.BlockSpec((CONV_K, SSD_W), lambda b, g: (0, 2304 // SSD_W + g)),
            pl.BlockSpec((CONV_K, LANE), lambda b, g: (0, 3072 // LANE + g)),
            pl.BlockSpec((CONV_K, LANE), lambda b, g: (0, 3328 // LANE + g)),
            pl.BlockSpec((1, SSD_W), lambda b, g: (0, 2304 // SSD_W + g)),
            pl.BlockSpec((1, LANE), lambda b, g: (0, 3072 // LANE + g)),
            pl.BlockSpec((1, LANE), lambda b, g: (0, 3328 // LANE + g)),
            pl.BlockSpec((1, SSD_W), lambda b, g: (0, g))],
        out_specs=pl.BlockSpec((1, L, SSD_W), lambda b, g: (b, 0, g)),
        scratch_shapes=[pltpu.VMEM((SSD_W // LANE, L + 2 * CONV_HALO, LANE), f32),
                        pltpu.VMEM((1, L + 2 * CONV_HALO, LANE), f32),
                        pltpu.VMEM((1, L + 2 * CONV_HALO, LANE), f32),
                        pltpu.VMEM((L, SSD_W), f32),
                        pltpu.VMEM((L, LANE), f32),
                        pltpu.VMEM((L, LANE), f32),
                        pltpu.VMEM((L, SSD_W), f32)],
        compiler_params=_cparams(("arbitrary", "arbitrary"), 48),
        name="ssd_mixer",
    )(proj, proj, proj, proj, *prep, conv_w, conv_w, conv_w, conv_b, conv_b, conv_b, dskip_row)


NA_WIN = NA_KH * GRID_W
NA_UNROLL = 4


def _na_kernel(q_ref, k_ref, v_ref, bias_ref, o_ref, kb, vb):
    L = q_ref.shape[1]
    rows = L // GRID_W
    kb[...] = k_ref[0].astype(bf16)
    vb[...] = v_ref[0].astype(bf16)

    def body(r, carry):
        low = lax.broadcasted_iota(i32, (GRID_W, LANE), 1) < NA_HEADDIM
        rs = jnp.clip(r - NA_KH // 2, 0, rows - NA_KH)
        off = r - rs
        q = q_ref[0, pl.ds(pl.multiple_of(r * GRID_W, GRID_W), GRID_W), :] * (NA_HEADDIM ** -0.5)
        w0 = pl.multiple_of(rs * GRID_W, GRID_W)
        kw = kb[pl.ds(w0, NA_WIN), :]
        vw = vb[pl.ds(w0, NA_WIN), :]
        outs = []
        for e in (0, 1):
            qm = jnp.where(low if e == 0 else jnp.logical_not(low), q, 0.0)
            s = _dot_nt(qm, kw) + bias_ref[e, off]
            p = jnp.exp(s - jnp.max(s, axis=-1, keepdims=True))
            den = jnp.sum(p, axis=-1, keepdims=True)
            outs.append(jnp.dot(p.astype(bf16), vw, preferred_element_type=f32) / den)
        o_ref[0, pl.ds(pl.multiple_of(r * GRID_W, GRID_W), GRID_W), :] = (
            jnp.where(low, outs[0], outs[1]).astype(o_ref.dtype))
        return carry

    lax.fori_loop(0, rows, body, 0, unroll=NA_UNROLL)


def _na_bias_tables(rpb):
    depth = rpb.shape[0]
    o = np.arange(NA_KH)[:, None]
    i = np.arange(NA_KH)[None, :]
    dr_idx = (i - o) + NA_KH - 1
    cols = np.arange(GRID_W)
    col_start = np.clip(cols - NA_KW // 2, 0, GRID_W - NA_KW)
    in_win = (cols[None, :] >= col_start[:, None]) & (cols[None, :] < col_start[:, None] + NA_KW)
    dc_idx = np.clip(cols[None, :] - cols[:, None], -(NA_KW - 1), NA_KW - 1) + NA_KW - 1
    oh_r = (dr_idx[..., None] == np.arange(2 * NA_KH - 1)).astype(np.float32)
    oh_c = (dc_idx[..., None] == np.arange(2 * NA_KW - 1)).astype(np.float32)
    hi = lax.Precision.HIGHEST
    t = jnp.einsum('oir,lhrc->lhoic', oh_r, rpb.astype(f32), precision=hi)
    bias = jnp.einsum('lhoic,qkc->lhoqik', t, oh_c, precision=hi)
    bias = jnp.where(in_win[None, None, None, :, None, :], bias, -jnp.inf)
    return bias.reshape(depth, NA_HEADS, NA_KH, GRID_W, NA_WIN)


def _na_mixer(proj, bias_tabs, l):
    bsz, L, _ = proj.shape
    assert L // GRID_W >= NA_KH and (L // GRID_W) % NA_UNROLL == 0
    blk = lambda off: pl.BlockSpec((1, L, LANE), lambda b, p, o=off // LANE: (b, 0, o + p))
    return pl.pallas_call(
        _na_kernel,
        out_shape=jax.ShapeDtypeStruct((bsz, L, NA_WIDTH), bf16),
        grid=(bsz, NA_HEADS // 2),
        in_specs=[blk(OFF_NQ), blk(OFF_NK), blk(OFF_NV),
                  pl.BlockSpec((None, 2, NA_KH, GRID_W, NA_WIN), lambda b, p: (l, p, 0, 0, 0))],
        out_specs=pl.BlockSpec((1, L, LANE), lambda b, p: (b, 0, p)),
        scratch_shapes=[pltpu.VMEM((L, LANE), bf16), pltpu.VMEM((L, LANE), bf16)],
        compiler_params=_cparams(("arbitrary", "arbitrary"), 32),
        name="na_mixer",
    )(proj, proj, proj, bias_tabs)


def _outproj_kernel(ya_ref, yb_ref, yc_ref, sn_ref, wa_ref, wb_ref, wc_ref, x_ref, mod_ref, nw_ref,
                    xo_ref, h2_ref, ybn):
    _norm_mod_rows(yb_ref, sn_ref[...], 0.0, 0.0, ybn)
    y = (jnp.dot(ya_ref[...], wa_ref[...], preferred_element_type=f32)
         + jnp.dot(ybn[...], wb_ref[...], preferred_element_type=f32)
         + jnp.dot(yc_ref[...], wc_ref[...], preferred_element_type=f32))
    xo_ref[...] = x_ref[...] + mod_ref[0, 2:3, :] * y
    _norm_mod_rows(xo_ref, nw_ref[...], mod_ref[0, 4:5, :], mod_ref[0, 3:4, :], h2_ref)


def _outproj(ya, yb, yc, ssm_norm, w_out_all, x2, mod, nw, l, L, h2_dtype):
    T, D = x2.shape
    tm = 512
    per_b = L // tm
    const = lambda shape: pl.BlockSpec(shape, lambda i: (0, 0), pipeline_mode=pl.Buffered(1))
    wspec = lambda rows_, blk_idx: pl.BlockSpec((None, rows_, D), lambda i: (l, blk_idx, 0),
                                                pipeline_mode=pl.Buffered(1))
    rows = lambda w: pl.BlockSpec((tm, w), lambda i: (i, 0))
    nb_off = (GDN_WIDTH + SSM_WIDTH) // NA_WIDTH
    return pl.pallas_call(
        _outproj_kernel,
        out_shape=(jax.ShapeDtypeStruct((T, D), f32), jax.ShapeDtypeStruct((T, D), h2_dtype)),
        grid=(T // tm,),
        in_specs=[rows(GDN_WIDTH), rows(SSM_WIDTH), rows(NA_WIDTH), const((1, SSM_WIDTH)),
                  wspec(GDN_WIDTH, 0), wspec(SSM_WIDTH, 1), wspec(NA_WIDTH, nb_off),
                  rows(D), pl.BlockSpec((None, 1, 6, D), _mod_spec(l, per_b, 1)), const((1, D))],
        out_specs=(rows(D), rows(D)),
        scratch_shapes=[pltpu.VMEM((tm, SSM_WIDTH), bf16)],
        compiler_params=_cparams(("arbitrary",), 48),
        name="outproj",
    )(ya, yb, yc, ssm_norm, w_out_all, w_out_all, w_out_all, x2, mod, nw)


def _swiglu_acc(h_ref, w1_ref, w3_ref, w2_ref, acc_ref, first):
    h = h_ref[...]
    a = jnp.dot(h, w1_ref[...], preferred_element_type=f32)
    b = jnp.dot(h, w3_ref[...], preferred_element_type=f32)
    u = (a * _sigmoid(a) * b).astype(bf16)
    part = jnp.dot(u, w2_ref[...], preferred_element_type=f32)

    @pl.when(first)
    def _():
        acc_ref[...] = part

    @pl.when(jnp.logical_not(first))
    def _():
        acc_ref[...] += part


def _ffn_kernel(h_ref, w1_ref, w3_ref, w2_ref, x_ref, mod_ref, o_ref, acc_ref):
    f = pl.program_id(1)
    _swiglu_acc(h_ref, w1_ref, w3_ref, w2_ref, acc_ref, f == 0)

    @pl.when(f == pl.num_programs(1) - 1)
    def _():
        o_ref[...] = x_ref[...] + mod_ref[0, 5:6, :] * acc_ref[...]


def _ffn_dense(h2, w1_all, w3_all, w2_all, m, x2, mod, l, L):
    T, D = x2.shape
    F = w1_all.shape[2]
    tm, tf = 512, 512
    per_b = L // tm
    return pl.pallas_call(
        _ffn_kernel,
        out_shape=jax.ShapeDtypeStruct((T, D), f32),
        grid=(T // tm, F // tf),
        in_specs=[pl.BlockSpec((tm, D), lambda i, f: (i, 0)),
                  pl.BlockSpec((None, D, tf), lambda i, f: (m, 0, f)),
                  pl.BlockSpec((None, D, tf), lambda i, f: (m, 0, f)),
                  pl.BlockSpec((None, tf, D), lambda i, f: (m, f, 0)),
                  pl.BlockSpec((tm, D), lambda i, f: (i, 0)),
                  pl.BlockSpec((None, 1, 6, D), _mod_spec(l, per_b, 2))],
        out_specs=pl.BlockSpec((tm, D), lambda i, f: (i, 0)),
        scratch_shapes=[pltpu.VMEM((tm, D), f32)],
        compiler_params=_cparams(("arbitrary", "arbitrary"), 48),
        name="ffn_dense",
    )(h2, w1_all, w3_all, w2_all, x2, mod)


MOE_TM = 512


def _moe_ffn_kernel(te_ref, nu_ref, h_ref, w1_ref, w3_ref, w2_ref, o_ref, acc_ref):
    i = pl.program_id(0)
    f = pl.program_id(1)
    last = f == pl.num_programs(1) - 1
    used = i < nu_ref[0]

    @pl.when(used)
    def _():
        _swiglu_acc(h_ref, w1_ref, w3_ref, w2_ref, acc_ref, f == 0)

    @pl.when(jnp.logical_and(used, last))
    def _():
        o_ref[...] = acc_ref[...]

    @pl.when(jnp.logical_and(jnp.logical_not(used), last))
    def _():
        o_ref[...] = jnp.zeros(o_ref.shape, f32)


def _moe_ffn(hs, w1_all, w3_all, w2_all, m, tile_expert, n_used):
    P, D = hs.shape
    F = w1_all.shape[3]
    tm, tf = MOE_TM, 512
    nf = F // tf

    def fe(i, f, nu):
        return jnp.where(i < nu[0], f, nf - 1)

    grid_spec = pltpu.PrefetchScalarGridSpec(
        num_scalar_prefetch=2,
        grid=(P // tm, nf),
        in_specs=[pl.BlockSpec((tm, D), lambda i, f, te, nu: (jnp.minimum(i, nu[0] - 1), 0)),
                  pl.BlockSpec((None, None, D, tf), lambda i, f, te, nu: (m, te[i], 0, fe(i, f, nu))),
                  pl.BlockSpec((None, None, D, tf), lambda i, f, te, nu: (m, te[i], 0, fe(i, f, nu))),
                  pl.BlockSpec((None, None, tf, D), lambda i, f, te, nu: (m, te[i], fe(i, f, nu), 0))],
        out_specs=pl.BlockSpec((tm, D), lambda i, f, te, nu: (i, 0)),
        scratch_shapes=[pltpu.VMEM((tm, D), f32)])
    return pl.pallas_call(
        _moe_ffn_kernel,
        out_shape=jax.ShapeDtypeStruct((P, D), f32),
        grid_spec=grid_spec,
        compiler_params=_cparams(("arbitrary", "arbitrary"), 48),
        name="moe_ffn",
    )(tile_expert, n_used, hs, w1_all, w3_all, w2_all)


def _router_kernel(h_ref, wr_ref, o_ref):
    hh, hm = _split(h_ref[...], 2)
    wh, wm = _split(wr_ref[...], 2)
    logits = (jnp.dot(hh, wh, preferred_element_type=f32) + jnp.dot(hh, wm, preferred_element_type=f32)
              + jnp.dot(hm, wh, preferred_element_type=f32))
    lane = lax.broadcasted_iota(i32, logits.shape, 1)
    logits = jnp.where(lane < N_EXPERTS, logits, -jnp.inf)
    m1 = jnp.max(logits, axis=-1, keepdims=True)
    i1 = jnp.min(jnp.where(logits == m1, lane, LANE), axis=-1, keepdims=True)
    rest = jnp.where(lane == i1, -jnp.inf, logits)
    m2 = jnp.max(rest, axis=-1, keepdims=True)
    i2 = jnp.min(jnp.where(rest == m2, lane, LANE), axis=-1, keepdims=True)
    e2 = jnp.exp(m2 - m1)
    g1 = 1.0 / (1.0 + e2)
    g2 = e2 / (1.0 + e2)
    o_ref[...] = jnp.where(lane == 0, g1, jnp.where(lane == 1, g2, jnp.where(
        lane == 2, i1.astype(f32), jnp.where(lane == 3, i2.astype(f32), 0.0))))


def _router(h2, wr_pad_all, m):
    T, D = h2.shape
    tm = 512
    return pl.pallas_call(
        _router_kernel,
        out_shape=jax.ShapeDtypeStruct((T, LANE), f32),
        grid=(T // tm,),
        in_specs=[pl.BlockSpec((tm, D), lambda i: (i, 0)),
                  pl.BlockSpec((None, D, LANE), lambda i: (m, 0, 0))],
        out_specs=pl.BlockSpec((tm, LANE), lambda i: (i, 0)),
        compiler_params=_cparams(("arbitrary",), 32),
        name="moe_router",
    )(h2, wr_pad_all)


GATHER_ROWS = 256


def _row_copy(src_hbm, buf, sem, t, r):
    return pltpu.make_async_copy(src_hbm.at[pl.ds(t, 1), :], buf.at[pl.ds(r, 1), :], sem)


ROW_UNROLL = 8


def _issue_rows(idx_ref, src_hbm, buf, sem):
    def issue(r, carry):
        _row_copy(src_hbm, buf, sem, idx_ref[0, 0, r], r).start()
        return carry

    lax.fori_loop(0, buf.shape[0], issue, 0, unroll=ROW_UNROLL)


def _wait_rows(src_hbm, buf, sem):
    def wait(r, carry):
        _row_copy(src_hbm, buf, sem, 0, r).wait()
        return carry

    lax.fori_loop(0, buf.shape[0], wait, 0, unroll=ROW_UNROLL)


def _prefetched_gather(step_idx_refs, next_idx_refs, src_hbm, bufs, sems):
    i = pl.program_id(0)
    slot = lax.rem(i, 2)
    for idx_ref, buf, sem in zip(step_idx_refs, bufs, sems):
        @pl.when(i == 0)
        def _(idx_ref=idx_ref, buf=buf, sem=sem):
            _issue_rows(idx_ref, src_hbm, buf.at[0], sem.at[0])
    for idx_ref, buf, sem in zip(next_idx_refs, bufs, sems):
        @pl.when(i + 1 < pl.num_programs(0))
        def _(idx_ref=idx_ref, buf=buf, sem=sem):
            _issue_rows(idx_ref, src_hbm, buf.at[1 - slot], sem.at[1 - slot])
    for buf, sem in zip(bufs, sems):
        _wait_rows(src_hbm, buf.at[slot], sem.at[slot])
    return slot


def _gather_kernel(idx_ref, nxt_ref, src_hbm, o_ref, buf, sem):
    slot = _prefetched_gather([idx_ref], [nxt_ref], src_hbm, [buf], [sem])
    o_ref[...] = buf[slot].astype(o_ref.dtype)


def _idx_specs(n_tiles, rt):
    cur = pl.BlockSpec((1, 1, rt), lambda i: (i, 0, 0), memory_space=pltpu.SMEM)
    nxt = pl.BlockSpec((1, 1, rt), lambda i: (jnp.minimum(i + 1, n_tiles - 1), 0, 0),
                       memory_space=pltpu.SMEM)
    return cur, nxt


def _gather_rows(src, idx, out_dtype):
    P = idx.shape[0]
    D = src.shape[1]
    rt = GATHER_ROWS
    cur, nxt = _idx_specs(P // rt, rt)
    idx3 = idx.reshape(P // rt, 1, rt)
    return pl.pallas_call(
        _gather_kernel,
        out_shape=jax.ShapeDtypeStruct((P, D), out_dtype),
        grid=(P // rt,),
        in_specs=[cur, nxt, pl.BlockSpec(memory_space=pl.ANY)],
        out_specs=pl.BlockSpec((rt, D), lambda i: (i, 0)),
        scratch_shapes=[pltpu.VMEM((2, rt, D), f32), pltpu.SemaphoreType.DMA((2,))],
        compiler_params=_cparams(("arbitrary",), 32),
        name="moe_gather",
    )(idx3, idx3, src)


def _combine_kernel(p0_ref, p0n_ref, p1_ref, p1n_ref, ys_hbm, x_ref, gt_ref, mod_ref, nf_ref, o_ref,
                    b0, b1, sem0, sem1, *, final_norm):
    slot = _prefetched_gather([p0_ref, p1_ref], [p0n_ref, p1n_ref], ys_hbm, [b0, b1], [sem0, sem1])
    gt = gt_ref[...]
    y = gt[:, 0:1] * b0[slot] + gt[:, 1:2] * b1[slot]
    xn = x_ref[...] + mod_ref[0, 5:6, :] * y
    if final_norm:
        xn = _rms(xn) * nf_ref[...]
    o_ref[...] = xn


def _moe_combine(ys, pos0, pos1, x2, gates, mod, norm_final, l, L, final_norm):
    T, D = x2.shape
    rt = GATHER_ROWS
    per_b = L // rt
    cur, nxt = _idx_specs(T // rt, rt)
    p0 = pos0.reshape(T // rt, 1, rt)
    p1 = pos1.reshape(T // rt, 1, rt)
    return pl.pallas_call(
        functools.partial(_combine_kernel, final_norm=final_norm),
        out_shape=jax.ShapeDtypeStruct((T, D), f32),
        grid=(T // rt,),
        in_specs=[cur, nxt, cur, nxt, pl.BlockSpec(memory_space=pl.ANY),
                  pl.BlockSpec((rt, D), lambda i: (i, 0)),
                  pl.BlockSpec((rt, LANE), lambda i: (i, 0)),
                  pl.BlockSpec((None, 1, 6, D), _mod_spec(l, per_b, 1)),
                  pl.BlockSpec((1, D), lambda i: (0, 0))],
        out_specs=pl.BlockSpec((rt, D), lambda i: (i, 0)),
        scratch_shapes=[pltpu.VMEM((2, rt, D), f32), pltpu.VMEM((2, rt, D), f32),
                        pltpu.SemaphoreType.DMA((2,)), pltpu.SemaphoreType.DMA((2,))],
        compiler_params=_cparams(("arbitrary",), 40),
        name="moe_combine",
    )(p0, p0, p1, p1, ys, x2, gates, mod, norm_final)


def _moe_plan(route, T):
    tm = MOE_TM
    P = 2 * T + N_EXPERTS * tm
    e_flat = route[:, 2:4].astype(i32).reshape(-1)
    onehot = (e_flat[:, None] == jnp.arange(N_EXPERTS, dtype=i32)[None, :]).astype(i32)
    csum = jnp.cumsum(onehot, axis=0)
    counts = csum[-1]
    rank = jnp.sum(csum * onehot, axis=1) - 1
    tiles = (counts + tm - 1) // tm
    tile_end = jnp.cumsum(tiles)
    off = (tile_end - tiles) * tm
    pos = off[e_flat] + rank
    tok = jnp.arange(2 * T, dtype=i32) // 2
    src = jnp.zeros((P,), i32).at[pos].set(tok, unique_indices=True)
    n_used = tile_end[-1:]
    tidx = jnp.arange(P // tm, dtype=i32)
    te = jnp.sum((tidx[:, None] >= tile_end[None, :]).astype(i32), axis=1)
    te_last = jnp.sum((n_used - 1 >= tile_end).astype(i32))
    te = jnp.where(tidx < n_used, jnp.minimum(te, N_EXPERTS - 1), te_last).astype(i32)
    pos2 = pos.reshape(T, 2)
    return src, pos2[:, 0], pos2[:, 1], te, n_used.astype(i32)


def _permute_w_in(w_in):
    depth, d, _ = w_in.shape
    pad = jnp.zeros((depth, d, PROJ_W - OFF_SM - 48), w_in.dtype)
    parts = [w_in[:, :, 0:3072], w_in[:, :, 4376:5144], w_in[:, :, 3584:4352], w_in[:, :, 3072:3584],
             w_in[:, :, 5168:6704], w_in[:, :, 4352:4376], w_in[:, :, 5144:5168], pad]
    return jnp.concatenate(parts, axis=-1).astype(bf16)


def _small_rows(gdn_p, ssm_p):
    depth = gdn_p.shape[0]
    row = jnp.zeros((depth, LANE), f32)
    row = row.at[:, SM_GA:SM_GA + 2 * GDN_HEADS].set(gdn_p.reshape(depth, -1).astype(f32))
    row = row.at[:, SM_DT:SM_DT + 2 * SSM_HEADS].set(ssm_p.reshape(depth, -1).astype(f32))
    return row.reshape(depth, 1, LANE)


def kernel(x, c, ada_w, ada_b, norm_mix, norm_ffn, norm_final, w_in, conv_w, conv_b, gdn_A_log, gdn_dt_bias, gdn_norm, ssm_A_log, ssm_dt_bias, ssm_D, ssm_norm, na_rpb, w_out, ffn_w1, ffn_w3, ffn_w2, moe_router, moe_w1, moe_w3, moe_w2):
    bsz, L, D = x.shape
    depth = w_in.shape[0]
    T = bsz * L

    mod = _ada_mod(c, ada_w, ada_b).reshape(depth, bsz, 6, D)
    w_in_p = _permute_w_in(w_in)
    w_out_b = w_out.astype(bf16)
    alog_rows = _small_rows(gdn_A_log, ssm_A_log)
    dtb_rows = _small_rows(gdn_dt_bias, ssm_dt_bias)
    dskip_rows = jnp.repeat(ssm_D.astype(f32), SSM_HEADDIM, axis=-1).reshape(depth, 1, SSM_WIDTH)
    bias_tabs = _na_bias_tables(na_rpb)
    ffn_w1b, ffn_w3b, ffn_w2b = ffn_w1.astype(bf16), ffn_w3.astype(bf16), ffn_w2.astype(bf16)
    moe_w1b, moe_w3b, moe_w2b = moe_w1.astype(bf16), moe_w3.astype(bf16), moe_w2.astype(bf16)
    wr_pad = jnp.pad(moe_router.astype(f32), ((0, 0), (0, 0), (0, LANE - N_EXPERTS)))

    x2 = x.reshape(T, D)
    for l in range(depth):
        is_moe = l % 2 == 1
        proj = _inproj(x2, mod, norm_mix[l].reshape(1, D), w_in_p, l, L).reshape(bsz, L, PROJ_W)
        prep = _mix_prep(proj, alog_rows[l], dtb_rows[l])
        cb = conv_b[l].reshape(1, CONV_CH)
        ya = _gdn_mixer(proj, prep, conv_w[l], cb, gdn_norm[l].reshape(1, LANE))
        yb = _ssd_mixer(proj, prep, conv_w[l], cb, dskip_rows[l])
        yc = _na_mixer(proj, bias_tabs, l)
        x2, h2 = _outproj(ya.reshape(T, GDN_WIDTH), yb.reshape(T, SSM_WIDTH), yc.reshape(T, NA_WIDTH),
                          ssm_norm[l].reshape(1, SSM_WIDTH), w_out_b, x2, mod, norm_ffn[l].reshape(1, D),
                          l, L, f32 if is_moe else bf16)
        m = l // 2
        if not is_moe:
            x2 = _ffn_dense(h2, ffn_w1b, ffn_w3b, ffn_w2b, m, x2, mod, l, L)
        else:
            route = _router(h2, wr_pad, m)
            src, pos0, pos1, te, n_used = _moe_plan(route, T)
            hs = _gather_rows(h2, src, bf16)
            ys = _moe_ffn(hs, moe_w1b, moe_w3b, moe_w2b, m, te, n_used)
            x2 = _moe_combine(ys, pos0, pos1, x2, route, mod, norm_final.reshape(1, D), l, L,
                              final_norm=(l == depth - 1))
    if depth % 2 == 1:
        raise NotImplementedError("the final norm is fused into the last MoE combine")
    return x2.reshape(bsz, L, D)
```

```python
import functools
import math

import jax
import jax.numpy as jnp
import numpy as np
from jax import lax
from jax.experimental import pallas as pl
from jax.experimental.pallas import tpu as pltpu

f32 = jnp.float32
bf16 = jnp.bfloat16
i32 = jnp.int32

GRID_W = 64
GDN_HEADS = 6
GDN_DK = 128
GDN_WIDTH = 768
SSM_HEADS = 12
SSM_HEADDIM = 64
SSM_WIDTH = 768
SSM_GROUPS = 2
SSM_HPG = 6
SSM_STATE = 128
NA_HEADS = 8
NA_HEADDIM = 64
NA_WIDTH = 512
NA_KH = 8
NA_KW = 16
CONV_K = 5
CONV_CH = 3584
N_EXPERTS = 8
EPS = 1e-6

OFF_GQ, OFF_GK, OFF_GV, OFF_SX, OFF_SZ, OFF_GZ = 0, 768, 1536, 2304, 3072, 3840
OFF_SB, OFF_SC, OFF_NQ, OFF_NK, OFF_NV, OFF_SM = 4608, 4864, 5120, 5632, 6144, 6656
PROJ_W = 6912
SM_BETA, SM_GA, SM_DT = 0, 12, 24

LANE = 128
MIX_CHUNK = 128
CONV_HALO = 8
MIB = 1024 * 1024


def _cparams(sem, vmem_mib):
    return pltpu.CompilerParams(dimension_semantics=sem, vmem_limit_bytes=vmem_mib * MIB)


def _sigmoid(x):
    return 1.0 / (1.0 + jnp.exp(-x))


def _softplus(x):
    return jnp.maximum(x, 0.0) + jnp.log1p(jnp.exp(-jnp.abs(x)))


def _dot(a, b):
    return jnp.dot(a.astype(bf16), b.astype(bf16), preferred_element_type=f32)


def _dot_nt(a, b):
    return lax.dot_general(a.astype(bf16), b.astype(bf16), (((1,), (1,)), ((), ())),
                           preferred_element_type=f32)


def _split(x, terms):
    out = []
    for _ in range(terms - 1):
        piece = x.astype(bf16)
        out.append(piece)
        x = x - piece.astype(f32)
    out.append(x.astype(bf16))
    return out


def _sel_left(m01, x, terms=3):
    m = m01.astype(bf16)
    return sum(jnp.dot(m, t, preferred_element_type=f32) for t in _split(x, terms))


def _sel_right(x, m01, terms=2):
    m = m01.astype(bf16)
    return sum(jnp.dot(t, m, preferred_element_type=f32) for t in _split(x, terms))


def _rms(x):
    return x * lax.rsqrt(jnp.mean(x * x, axis=-1, keepdims=True) + EPS)


def _conv_silu(xp_ref, cw_ref, cb_ref, r0, rows, blk=0):
    lanes = slice(blk * LANE, (blk + 1) * LANE)
    acc = jnp.broadcast_to(cb_ref[:, lanes], (rows, LANE))
    for j in range(CONV_K):
        shift = CONV_HALO - CONV_K // 2 + j
        acc = acc + cw_ref[j:j + 1, lanes] * xp_ref[blk, pl.ds(r0 + shift, rows), :]
    return acc * _sigmoid(acc)


def _fill_padded(src_ref, dst_ref, L):
    for blk in range(dst_ref.shape[0]):
        dst_ref[blk, 0:CONV_HALO, :] = jnp.zeros((CONV_HALO, LANE), f32)
        dst_ref[blk, L + CONV_HALO:L + 2 * CONV_HALO, :] = jnp.zeros((CONV_HALO, LANE), f32)
        dst_ref[blk, CONV_HALO:L + CONV_HALO, :] = src_ref[0, :, blk * LANE:(blk + 1) * LANE]


def _pick_col(blk, idx):
    lane = lax.broadcasted_iota(i32, blk.shape, 1)
    return jnp.sum(jnp.where(lane == idx, blk, 0.0), axis=-1, keepdims=True)


def _ada_kernel(c_ref, w_ref, b_ref, o_ref):
    c = c_ref[...]
    cond = c * _sigmoid(c)
    o_ref[0] = _dot(cond, w_ref[0]) + b_ref[0]


def _ada_mod(c, ada_w, ada_b):
    depth, d, n = ada_w.shape
    bsz = c.shape[0]
    tn = 1536
    return pl.pallas_call(
        _ada_kernel,
        out_shape=jax.ShapeDtypeStruct((depth, bsz, n), f32),
        grid=(depth, n // tn),
        in_specs=[pl.BlockSpec((bsz, d), lambda l, j: (0, 0)),
                  pl.BlockSpec((1, d, tn), lambda l, j: (l, 0, j)),
                  pl.BlockSpec((1, 1, tn), lambda l, j: (l, 0, j))],
        out_specs=pl.BlockSpec((1, bsz, tn), lambda l, j: (l, 0, j)),
        compiler_params=_cparams(("arbitrary", "arbitrary"), 40),
        name="ada_mod",
    )(c, ada_w, ada_b.reshape(depth, 1, n))


NORM_ROWS = 128


def _norm_mod_rows(x_ref, nw, sc, sh, h_ref):
    def body(i, carry):
        r = pl.multiple_of(i * NORM_ROWS, NORM_ROWS)
        y = _rms(x_ref[pl.ds(r, NORM_ROWS), :]) * nw
        h_ref[pl.ds(r, NORM_ROWS), :] = (y * (1.0 + sc) + sh).astype(h_ref.dtype)
        return carry
    lax.fori_loop(0, x_ref.shape[0] // NORM_ROWS, body, 0)


def _inproj_kernel(x_ref, mod_ref, nw_ref, w_ref, o_ref, h_ref):
    @pl.when(pl.program_id(1) == 0)
    def _():
        _norm_mod_rows(x_ref, nw_ref[...], mod_ref[0, 1:2, :], mod_ref[0, 0:1, :], h_ref)
    o_ref[...] = jnp.dot(h_ref[...], w_ref[...], preferred_element_type=f32)


def _mod_spec(l, per_b, nargs):
    if nargs == 1:
        return lambda i: (l, i // per_b, 0, 0)
    return lambda i, j: (l, i // per_b, 0, 0)


def _inproj(x2, mod, nw, w_all, l, L):
    T, D = x2.shape
    NP = w_all.shape[2]
    tm, tn = min(1024, L), 768
    per_b = L // tm
    return pl.pallas_call(
        _inproj_kernel,
        out_shape=jax.ShapeDtypeStruct((T, NP), f32),
        grid=(T // tm, NP // tn),
        in_specs=[pl.BlockSpec((tm, D), lambda i, j: (i, 0)),
                  pl.BlockSpec((None, 1, 6, D), _mod_spec(l, per_b, 2)),
                  pl.BlockSpec((1, D), lambda i, j: (0, 0)),
                  pl.BlockSpec((None, D, tn), lambda i, j: (l, 0, j))],
        out_specs=pl.BlockSpec((tm, tn), lambda i, j: (i, j)),
        scratch_shapes=[pltpu.VMEM((tm, D), bf16)],
        compiler_params=_cparams(("arbitrary", "arbitrary"), 44),
        name="inproj",
    )(x2, mod, nw, w_all)


def _prep_kernel(sm_ref, alog_ref, dtb_ref, act_ref, cum_ref, cumt_ref, tot_ref):
    L = sm_ref.shape[1]
    C = MIX_CHUNK
    ri = lax.broadcasted_iota(i32, (C, C), 0)
    ci = lax.broadcasted_iota(i32, (C, C), 1)
    trilf = jnp.where(ri >= ci, 1.0, 0.0)
    lane = lax.broadcasted_iota(i32, (C, LANE), 1)
    gdn_rev = jnp.logical_and(lane >= SM_GA + GDN_HEADS, lane < SM_DT)
    backward = jnp.logical_or(gdn_rev, lane >= SM_DT + SSM_HEADS)
    neg_a = -jnp.exp(alog_ref[...])
    dtb = dtb_ref[...]

    def body(c, carry):
        r0 = pl.multiple_of(c * C, C)
        sm = sm_ref[0, pl.ds(r0, C), :]
        sp = _softplus(sm + dtb)
        act_ref[0, pl.ds(r0, C), :] = jnp.where(lane < SM_GA, _sigmoid(sm), sp)
        a = neg_a * sp
        cum = _sel_left(trilf, a)
        tot = cum[C - 1:C, :]
        cd = jnp.where(backward, tot - cum + a, cum)
        cum_ref[0, pl.ds(r0, C), :] = cd
        cumt_ref[0, c] = cd.T
        tot_ref[0, c] = jnp.broadcast_to(tot, (8, LANE))
        return carry

    lax.fori_loop(0, L // C, body, 0)


def _mix_prep(proj, alog_row, dtb_row):
    bsz, L, _ = proj.shape
    C = MIX_CHUNK
    nc = L // C
    row = pl.BlockSpec((1, LANE), lambda b: (0, 0))
    tok = pl.BlockSpec((1, L, LANE), lambda b: (b, 0, 0))
    return pl.pallas_call(
        _prep_kernel,
        out_shape=(jax.ShapeDtypeStruct((bsz, L, LANE), f32), jax.ShapeDtypeStruct((bsz, L, LANE), f32),
                   jax.ShapeDtypeStruct((bsz, nc, LANE, C), f32), jax.ShapeDtypeStruct((bsz, nc, 8, LANE), f32)),
        grid=(bsz,),
        in_specs=[pl.BlockSpec((1, L, LANE), lambda b: (b, 0, OFF_SM // LANE)), row, row],
        out_specs=(tok, tok, pl.BlockSpec((1, nc, LANE, C), lambda b: (b, 0, 0, 0)),
                   pl.BlockSpec((1, nc, 8, LANE), lambda b: (b, 0, 0, 0))),
        compiler_params=_cparams(("arbitrary",), 32),
        name="mix_prep",
    )(proj, alog_row, dtb_row)


GDN_GROUP = 8
INV_BASE = 16


def _unit_tri_inverses(a_mats):
    C = a_mats[0].shape[0]
    ri = lax.broadcasted_iota(i32, (C, C), 0)
    ci = lax.broadcasted_iota(i32, (C, C), 1)
    eye = jnp.where(ri == ci, 1.0, 0.0)

    def same(b):
        s = int(math.log2(b))
        return lax.shift_right_logical(ri, s) == lax.shift_right_logical(ci, s)

    ps = [jnp.where(same(INV_BASE), a, 0.0) for a in a_mats]
    ts = [eye - p for p in ps]
    for _ in range(int(math.log2(INV_BASE)) - 1):
        ps = [_dot(p, p) for p in ps]
        ts = [t + _dot(t, p) for t, p in zip(ts, ps)]
    b = INV_BASE
    while b < C:
        off = jnp.logical_and(same(2 * b), jnp.logical_not(same(b)))
        us = [_dot(jnp.where(off, a, 0.0), t) for a, t in zip(a_mats, ts)]
        ts = [t - _dot(t, u) for t, u in zip(ts, us)]
        b *= 2
    return ts


def _gdn_kernel(q_ref, k_ref, v_ref, z_ref, act_ref, cum_ref, cumt_ref, tot_ref,
                cwq, cwk, cwv, cbq, cbk, cbv, nw_ref, o_ref,
                qp, kp, vp, kmat_s, nmat_s, qeff_s, gl_s, oacc):
    h = pl.program_id(1)
    L = q_ref.shape[1]
    C = MIX_CHUNK
    nc = L // C
    group = math.gcd(GDN_GROUP, nc)
    _fill_padded(q_ref, qp, L)
    _fill_padded(k_ref, kp, L)
    _fill_padded(v_ref, vp, L)
    oacc[...] = jnp.zeros(oacc.shape, f32)

    def phase_a(gi, carry):
        ri = lax.broadcasted_iota(i32, (C, C), 0)
        ci = lax.broadcasted_iota(i32, (C, C), 1)
        chains = []
        for j in range(group):
            c = gi * group + j
            r0 = pl.multiple_of(c * C, C)
            q = _conv_silu(qp, cwq, cbq, r0, C)
            k = _conv_silu(kp, cwk, cbk, r0, C)
            v = _conv_silu(vp, cwv, cbv, r0, C)
            qn = q * lax.rsqrt(jnp.sum(q * q, axis=-1, keepdims=True) + EPS) * (GDN_DK ** -0.5)
            kn = k * lax.rsqrt(jnp.sum(k * k, axis=-1, keepdims=True) + EPS)
            kk = _dot_nt(kn, kn)
            qk = _dot_nt(qn, kn)
            act = act_ref[0, pl.ds(r0, C), :]
            cum = cum_ref[0, pl.ds(r0, C), :]
            for d in (0, 1):
                gidx = SM_GA + d * GDN_HEADS + h
                beta = _pick_col(act, SM_BETA + d * GDN_HEADS + h)
                gcol = _pick_col(cum, gidx)
                grow = cumt_ref[0, c, pl.ds(gidx, 1), :]
                gl = _pick_col(tot_ref[0, c, 0:1, :], gidx)
                mask = (ri >= ci) if d == 0 else (ri <= ci)
                strict = (ri > ci) if d == 0 else (ri < ci)
                dec = jnp.exp(jnp.where(mask, gcol - grow, -jnp.inf))
                a_mat = jnp.where(strict, beta * kk * dec, 0.0)
                chains.append((c, r0, d, a_mat, dec, beta, gcol, gl, qn, kn, v, qk))
        t_invs = _unit_tri_inverses([ch[3] for ch in chains])
        for (c, r0, d, _, dec, beta, gcol, gl, qn, kn, v, qk), t_inv in zip(chains, t_invs):
            eg = jnp.exp(gcol)
            sol = _dot(t_inv, jnp.concatenate([v * beta, kn * (beta * eg)], axis=1))
            kd = kn * jnp.exp(gl - gcol)
            big = _dot(jnp.concatenate([kd.T, qk * dec], axis=0), sol)
            nmat_s[d, c] = big[0:GDN_DK, 0:LANE]
            kmat_s[d, c] = big[0:GDN_DK, LANE:].astype(bf16)
            qeff_s[d, pl.ds(r0, C), :] = (qn * eg - big[GDN_DK:, LANE:]).astype(bf16)
            oacc[pl.ds(r0, C), :] += big[GDN_DK:, 0:LANE]
            gl_s[d, c] = jnp.broadcast_to(jnp.exp(gl), (8, LANE))
        return carry

    lax.fori_loop(0, nc // group, phase_a, 0)

    def phase_b(i, carry):
        out = []
        for d in (0, 1):
            s = carry[d]
            c = i if d == 0 else nc - 1 - i
            r0 = pl.multiple_of(c * C, C)
            sb = s.astype(bf16)
            oacc[pl.ds(r0, C), :] += jnp.dot(qeff_s[d, pl.ds(r0, C), :], sb, preferred_element_type=f32)
            out.append(s * gl_s[d, c][0:1, :] - jnp.dot(kmat_s[d, c], sb, preferred_element_type=f32)
                       + nmat_s[d, c])
        return tuple(out)

    s0 = jnp.zeros((GDN_DK, LANE), f32)
    lax.fori_loop(0, nc, phase_b, (s0, s0))

    def fin(c, carry):
        r0 = pl.multiple_of(c * C, C)
        y = _rms(oacc[pl.ds(r0, C), :]) * nw_ref[...]
        z = z_ref[0, pl.ds(r0, C), :]
        o_ref[0, pl.ds(r0, C), :] = (y * (z * _sigmoid(z))).astype(o_ref.dtype)
        return carry

    lax.fori_loop(0, nc, fin, 0)


def _prep_specs(L, nc, C):
    tok = pl.BlockSpec((1, L, LANE), lambda b, h: (b, 0, 0))
    return [tok, tok, pl.BlockSpec((1, nc, LANE, C), lambda b, h: (b, 0, 0, 0)),
            pl.BlockSpec((1, nc, 8, LANE), lambda b, h: (b, 0, 0, 0))]


def _gdn_mixer(proj, prep, conv_w, conv_b, gdn_norm):
    bsz, L, _ = proj.shape
    C = MIX_CHUNK
    nc = L // C
    blk = lambda off: pl.BlockSpec((1, L, LANE), lambda b, h, o=off // LANE: (b, 0, o + h))
    cwb = lambda off: pl.BlockSpec((CONV_K, LANE), lambda b, h, o=off // LANE: (0, o + h))
    cbb = lambda off: pl.BlockSpec((1, LANE), lambda b, h, o=off // LANE: (0, o + h))
    return pl.pallas_call(
        _gdn_kernel,
        out_shape=jax.ShapeDtypeStruct((bsz, L, GDN_WIDTH), bf16),
        grid=(bsz, GDN_HEADS),
        in_specs=[blk(OFF_GQ), blk(OFF_GK), blk(OFF_GV), blk(OFF_GZ)] + _prep_specs(L, nc, C) + [
            cwb(0), cwb(768), cwb(1536), cbb(0), cbb(768), cbb(1536),
            pl.BlockSpec((1, LANE), lambda b, h: (0, 0))],
        out_specs=pl.BlockSpec((1, L, LANE), lambda b, h: (b, 0, h)),
        scratch_shapes=[pltpu.VMEM((1, L + 2 * CONV_HALO, LANE), f32)] * 3 + [
            pltpu.VMEM((2, nc, GDN_DK, LANE), bf16),
            pltpu.VMEM((2, nc, GDN_DK, LANE), f32),
            pltpu.VMEM((2, L, LANE), bf16),
            pltpu.VMEM((2, nc, 8, LANE), f32),
            pltpu.VMEM((L, LANE), f32)],
        compiler_params=_cparams(("arbitrary", "arbitrary"), 40),
        name="gdn_mixer",
    )(proj, proj, proj, proj, *prep, conv_w, conv_w, conv_w, conv_b, conv_b, conv_b, gdn_norm)


SSD_W = SSM_HPG * SSM_HEADDIM


def _ssd_kernel(x_ref, z_ref, b_ref, c_ref, act_ref, cum_ref, cumt_ref, tot_ref,
                cwx, cwb, cwc, cbx, cbb, cbc, dsk_ref, o_ref, xp, bp, cp, xc, bc, cc, yacc):
    g = pl.program_id(1)
    L = x_ref.shape[1]
    C = MIX_CHUNK
    nc = L // C
    _fill_padded(x_ref, xp, L)
    _fill_padded(b_ref, bp, L)
    _fill_padded(c_ref, cp, L)
    yacc[...] = jnp.zeros(yacc.shape, f32)

    def conv_all(c, carry):
        r0 = pl.multiple_of(c * C, C)
        for blk in range(SSD_W // LANE):
            xc[pl.ds(r0, C), blk * LANE:(blk + 1) * LANE] = _conv_silu(xp, cwx, cbx, r0, C, blk)
        bc[pl.ds(r0, C), :] = _conv_silu(bp, cwb, cbb, r0, C)
        cc[pl.ds(r0, C), :] = _conv_silu(cp, cwc, cbc, r0, C)
        return carry

    lax.fori_loop(0, nc, conv_all, 0)

    def step(i, carry):
        ri = lax.broadcasted_iota(i32, (C, C), 0)
        ci = lax.broadcasted_iota(i32, (C, C), 1)
        er = lax.broadcasted_iota(i32, (LANE, SSD_W), 0)
        el = lax.shift_right_logical(lax.broadcasted_iota(i32, (LANE, SSD_W), 1),
                                     int(math.log2(SSM_HEADDIM)))
        half = lax.broadcasted_iota(i32, (C, LANE), 1) < SSM_HEADDIM
        out = []
        for d in (0, 1):
            ht = carry[d]
            c = i if d == 0 else nc - 1 - i
            r0 = pl.multiple_of(c * C, C)
            base = SM_DT + d * SSM_HEADS + g * SSM_HPG
            expand = jnp.where(er == base + el, 1.0, 0.0)
            cum = cum_ref[0, pl.ds(r0, C), :]
            g6 = _sel_right(cum, expand)
            dt6 = _sel_right(act_ref[0, pl.ds(r0, C), :], expand)
            tot6 = _sel_right(tot_ref[0, c], expand)[0:1, :]
            xdt = xc[pl.ds(r0, C), :] * dt6
            bm = bc[pl.ds(r0, C), :]
            cm = cc[pl.ds(r0, C), :]
            cb = _dot_nt(cm, bm)
            mask = (ri >= ci) if d == 0 else (ri <= ci)
            pieces = []
            for pr in range(SSM_HPG // 2):
                xpair = xdt[:, pr * LANE:(pr + 1) * LANE].astype(bf16)
                ys = []
                for e in (0, 1):
                    gidx = base + 2 * pr + e
                    diff = _pick_col(cum, gidx) - cumt_ref[0, c, pl.ds(gidx, 1), :]
                    lm = jnp.exp(jnp.where(mask, diff, -jnp.inf))
                    ys.append(jnp.dot((cb * lm).astype(bf16), xpair, preferred_element_type=f32))
                pieces.append(jnp.where(half, ys[0], ys[1]))
            y_diag = jnp.concatenate(pieces, axis=1)
            y_off = _dot(cm, ht) * jnp.exp(g6)
            yacc[pl.ds(r0, C), :] += y_diag + y_off
            st = _dot(bm.T, xdt * jnp.exp(tot6 - g6))
            out.append(ht * jnp.exp(tot6) + st)
        return tuple(out)

    h0 = jnp.zeros((SSM_STATE, SSD_W), f32)
    lax.fori_loop(0, nc, step, (h0, h0), unroll=2)

    def fin(c, carry):
        r0 = pl.multiple_of(c * C, C)
        y = yacc[pl.ds(r0, C), :] + xc[pl.ds(r0, C), :] * dsk_ref[...]
        z = z_ref[0, pl.ds(r0, C), :]
        o_ref[0, pl.ds(r0, C), :] = y * (z * _sigmoid(z))
        return carry

    lax.fori_loop(0, nc, fin, 0)


def _ssd_mixer(proj, prep, conv_w, conv_b, dskip_row):
    bsz, L, _ = proj.shape
    C = MIX_CHUNK
    nc = L // C
    wide = lambda off: pl.BlockSpec((1, L, SSD_W), lambda b, g, o=off // SSD_W: (b, 0, o + g))
    nar = lambda off: pl.BlockSpec((1, L, LANE), lambda b, g, o=off // LANE: (b, 0, o + g))
    return pl.pallas_call(
        _ssd_kernel,
        out_shape=jax.ShapeDtypeStruct((bsz, L, SSM_WIDTH), f32),
        grid=(bsz, SSM_GROUPS),
        in_specs=[wide(OFF_SX), wide(OFF_SZ), nar(OFF_SB), nar(OFF_SC)] + _prep_specs(L, nc, C) + [
            pl.BlockSpec((CONV_K, SSD_W), lambda b, g: (0, 2304 // SSD_W + g)),
            pl.BlockSpec((CONV_K, LANE), lambda b, g: (0, 3072 // LANE + g)),
            pl.BlockSpec((CONV_K, LANE), lambda b, g: (0, 3328 // LANE + g)),
            pl.BlockSpec((1, SSD_W), lambda b, g: (0, 2304 // SSD_W + g)),
            pl.BlockSpec((1, LANE), lambda b, g: (0, 3072 // LANE + g)),
            pl.BlockSpec((1, LANE), lambda b, g: (0, 3328 // LANE + g)),
            pl.BlockSpec((1, SSD_W), lambda b, g: (0, g))],
        out_specs=pl.BlockSpec((1, L, SSD_W), lambda b, g: (b, 0, g)),
        scratch_shapes=[pltpu.VMEM((SSD_W // LANE, L + 2 * CONV_HALO, LANE), f32),
                        pltpu.VMEM((1, L + 2 * CONV_HALO, LANE), f32),
                        pltpu.VMEM((1, L + 2 * CONV_HALO, LANE), f32),
                        pltpu.VMEM((L, SSD_W), f32),
                        pltpu.VMEM((L, LANE), f32),
                        pltpu.VMEM((L, LANE), f32),
                        pltpu.VMEM((L, SSD_W), f32)],
        compiler_params=_cparams(("arbitrary", "arbitrary"), 48),
        name="ssd_mixer",
    )(proj, proj, proj, proj, *prep, conv_w, conv_w, conv_w, conv_b, conv_b, conv_b, dskip_row)


NA_WIN = NA_KH * GRID_W
NA_UNROLL = 4


def _na_kernel(q_ref, k_ref, v_ref, bias_ref, o_ref, kb, vb, s_scr, p_scr, den_scr):
    L = q_ref.shape[1]
    rows = L // GRID_W
    kb[...] = k_ref[0].astype(bf16)
    vb[...] = v_ref[0].astype(bf16)

    def body(g, carry):
        low = lax.broadcasted_iota(i32, (GRID_W, LANE), 1) < NA_HEADDIM
        wins = []
        for j in range(NA_UNROLL):
            r = g * NA_UNROLL + j
            rs = jnp.clip(r - NA_KH // 2, 0, rows - NA_KH)
            off = r - rs
            q0 = pl.multiple_of(r * GRID_W, GRID_W)
            w0 = pl.multiple_of(rs * GRID_W, GRID_W)
            q = q_ref[0, pl.ds(q0, GRID_W), :] * (NA_HEADDIM ** -0.5)
            kw = kb[pl.ds(w0, NA_WIN), :]
            for e in (0, 1):
                qm = jnp.where(low if e == 0 else jnp.logical_not(low), q, 0.0)
                s_scr[j, e] = _dot_nt(qm, kw) + bias_ref[e, off]
            wins.append((q0, w0))
        for j in range(NA_UNROLL):
            for e in (0, 1):
                s = s_scr[j, e]
                p = jnp.exp(s - jnp.max(s, axis=-1, keepdims=True))
                den_scr[j, e] = jnp.broadcast_to(jnp.sum(p, axis=-1, keepdims=True), (GRID_W, LANE))
                p_scr[j, e] = p.astype(bf16)
        for j, (q0, w0) in enumerate(wins):
            vw = vb[pl.ds(w0, NA_WIN), :]
            outs = [jnp.dot(p_scr[j, e], vw, preferred_element_type=f32) / den_scr[j, e] for e in (0, 1)]
            o_ref[0, pl.ds(q0, GRID_W), :] = jnp.where(low, outs[0], outs[1]).astype(o_ref.dtype)
        return carry

    lax.fori_loop(0, rows // NA_UNROLL, body, 0)


def _na_bias_tables(rpb):
    depth = rpb.shape[0]
    o = np.arange(NA_KH)[:, None]
    i = np.arange(NA_KH)[None, :]
    dr_idx = (i - o) + NA_KH - 1
    cols = np.arange(GRID_W)
    col_start = np.clip(cols - NA_KW // 2, 0, GRID_W - NA_KW)
    in_win = (cols[None, :] >= col_start[:, None]) & (cols[None, :] < col_start[:, None] + NA_KW)
    dc_idx = np.clip(cols[None, :] - cols[:, None], -(NA_KW - 1), NA_KW - 1) + NA_KW - 1
    oh_r = (dr_idx[..., None] == np.arange(2 * NA_KH - 1)).astype(np.float32)
    oh_c = (dc_idx[..., None] == np.arange(2 * NA_KW - 1)).astype(np.float32)
    hi = lax.Precision.HIGHEST
    t = jnp.einsum('oir,lhrc->lhoic', oh_r, rpb.astype(f32), precision=hi)
    bias = jnp.einsum('lhoic,qkc->lhoqik', t, oh_c, precision=hi)
    bias = jnp.where(in_win[None, None, None, :, None, :], bias, -jnp.inf)
    return bias.reshape(depth, NA_HEADS, NA_KH, GRID_W, NA_WIN)


def _na_mixer(proj, bias_tabs, l):
    bsz, L, _ = proj.shape
    assert L // GRID_W >= NA_KH and (L // GRID_W) % NA_UNROLL == 0
    blk = lambda off: pl.BlockSpec((1, L, LANE), lambda b, p, o=off // LANE: (b, 0, o + p))
    return pl.pallas_call(
        _na_kernel,
        out_shape=jax.ShapeDtypeStruct((bsz, L, NA_WIDTH), bf16),
        grid=(bsz, NA_HEADS // 2),
        in_specs=[blk(OFF_NQ), blk(OFF_NK), blk(OFF_NV),
                  pl.BlockSpec((None, 2, NA_KH, GRID_W, NA_WIN), lambda b, p: (l, p, 0, 0, 0))],
        out_specs=pl.BlockSpec((1, L, LANE), lambda b, p: (b, 0, p)),
        scratch_shapes=[pltpu.VMEM((L, LANE), bf16), pltpu.VMEM((L, LANE), bf16),
                        pltpu.VMEM((NA_UNROLL, 2, GRID_W, NA_WIN), f32),
                        pltpu.VMEM((NA_UNROLL, 2, GRID_W, NA_WIN), bf16),
                        pltpu.VMEM((NA_UNROLL, 2, GRID_W, LANE), f32)],
        compiler_params=_cparams(("arbitrary", "arbitrary"), 32),
        name="na_mixer",
    )(proj, proj, proj, bias_tabs)


def _outproj_kernel(ya_ref, yb_ref, yc_ref, sn_ref, wa_ref, wb_ref, wc_ref, x_ref, mod_ref, nw_ref,
                    xo_ref, h2_ref, ybn):
    _norm_mod_rows(yb_ref, sn_ref[...], 0.0, 0.0, ybn)
    y = (jnp.dot(ya_ref[...], wa_ref[...], preferred_element_type=f32)
         + jnp.dot(ybn[...], wb_ref[...], preferred_element_type=f32)
         + jnp.dot(yc_ref[...], wc_ref[...], preferred_element_type=f32))
    xo_ref[...] = x_ref[...] + mod_ref[0, 2:3, :] * y
    _norm_mod_rows(xo_ref, nw_ref[...], mod_ref[0, 4:5, :], mod_ref[0, 3:4, :], h2_ref)


def _outproj(ya, yb, yc, ssm_norm, w_out_all, x2, mod, nw, l, L, h2_dtype):
    T, D = x2.shape
    tm = 512
    per_b = L // tm
    const = lambda shape: pl.BlockSpec(shape, lambda i: (0, 0), pipeline_mode=pl.Buffered(1))
    wspec = lambda rows_, blk_idx: pl.BlockSpec((None, rows_, D), lambda i: (l, blk_idx, 0),
                                                pipeline_mode=pl.Buffered(1))
    rows = lambda w: pl.BlockSpec((tm, w), lambda i: (i, 0))
    nb_off = (GDN_WIDTH + SSM_WIDTH) // NA_WIDTH
    return pl.pallas_call(
        _outproj_kernel,
        out_shape=(jax.ShapeDtypeStruct((T, D), f32), jax.ShapeDtypeStruct((T, D), h2_dtype)),
        grid=(T // tm,),
        in_specs=[rows(GDN_WIDTH), rows(SSM_WIDTH), rows(NA_WIDTH), const((1, SSM_WIDTH)),
                  wspec(GDN_WIDTH, 0), wspec(SSM_WIDTH, 1), wspec(NA_WIDTH, nb_off),
                  rows(D), pl.BlockSpec((None, 1, 6, D), _mod_spec(l, per_b, 1)), const((1, D))],
        out_specs=(rows(D), rows(D)),
        scratch_shapes=[pltpu.VMEM((tm, SSM_WIDTH), bf16)],
        compiler_params=_cparams(("arbitrary",), 48),
        name="outproj",
    )(ya, yb, yc, ssm_norm, w_out_all, w_out_all, w_out_all, x2, mod, nw)


def _swiglu_acc(h_ref, w1_ref, w3_ref, w2_ref, acc_ref, first):
    @pl.when(first)
    def _():
        acc_ref[...] = jnp.zeros(acc_ref.shape, f32)

    h = h_ref[...]
    a = jnp.dot(h, w1_ref[...], preferred_element_type=f32)
    b = jnp.dot(h, w3_ref[...], preferred_element_type=f32)
    u = (a * _sigmoid(a) * b).astype(bf16)
    acc_ref[...] += jnp.dot(u, w2_ref[...], preferred_element_type=f32)


def _ffn_kernel(h_ref, w1_ref, w3_ref, w2_ref, x_ref, mod_ref, o_ref, acc_ref):
    f = pl.program_id(1)
    _swiglu_acc(h_ref, w1_ref, w3_ref, w2_ref, acc_ref, f == 0)

    @pl.when(f == pl.num_programs(1) - 1)
    def _():
        o_ref[...] = x_ref[...] + mod_ref[0, 5:6, :] * acc_ref[...]


def _ffn_dense(h2, w1_all, w3_all, w2_all, m, x2, mod, l, L):
    T, D = x2.shape
    F = w1_all.shape[2]
    tm, tf = 512, 512
    per_b = L // tm
    return pl.pallas_call(
        _ffn_kernel,
        out_shape=jax.ShapeDtypeStruct((T, D), f32),
        grid=(T // tm, F // tf),
        in_specs=[pl.BlockSpec((tm, D), lambda i, f: (i, 0)),
                  pl.BlockSpec((None, D, tf), lambda i, f: (m, 0, f)),
                  pl.BlockSpec((None, D, tf), lambda i, f: (m, 0, f)),
                  pl.BlockSpec((None, tf, D), lambda i, f: (m, f, 0)),
                  pl.BlockSpec((tm, D), lambda i, f: (i, 0)),
                  pl.BlockSpec((None, 1, 6, D), _mod_spec(l, per_b, 2))],
        out_specs=pl.BlockSpec((tm, D), lambda i, f: (i, 0)),
        scratch_shapes=[pltpu.VMEM((tm, D), f32)],
        compiler_params=_cparams(("arbitrary", "arbitrary"), 48),
        name="ffn_dense",
    )(h2, w1_all, w3_all, w2_all, x2, mod)


MOE_TM = 512


def _moe_ffn_kernel(te_ref, nu_ref, h_ref, w1_ref, w3_ref, w2_ref, o_ref, acc_ref):
    i = pl.program_id(0)
    f = pl.program_id(1)
    last = f == pl.num_programs(1) - 1
    used = i < nu_ref[0]

    @pl.when(used)
    def _():
        _swiglu_acc(h_ref, w1_ref, w3_ref, w2_ref, acc_ref, f == 0)

    @pl.when(jnp.logical_and(used, last))
    def _():
        o_ref[...] = acc_ref[...]

    @pl.when(jnp.logical_and(jnp.logical_not(used), last))
    def _():
        o_ref[...] = jnp.zeros(o_ref.shape, f32)


def _moe_ffn(hs, w1_all, w3_all, w2_all, m, tile_expert, n_used):
    P, D = hs.shape
    F = w1_all.shape[3]
    tm, tf = MOE_TM, 512
    nf = F // tf

    def fe(i, f, nu):
        return jnp.where(i < nu[0], f, nf - 1)

    grid_spec = pltpu.PrefetchScalarGridSpec(
        num_scalar_prefetch=2,
        grid=(P // tm, nf),
        in_specs=[pl.BlockSpec((tm, D), lambda i, f, te, nu: (jnp.minimum(i, nu[0] - 1), 0)),
                  pl.BlockSpec((None, None, D, tf), lambda i, f, te, nu: (m, te[i], 0, fe(i, f, nu))),
                  pl.BlockSpec((None, None, D, tf), lambda i, f, te, nu: (m, te[i], 0, fe(i, f, nu))),
                  pl.BlockSpec((None, None, tf, D), lambda i, f, te, nu: (m, te[i], fe(i, f, nu), 0))],
        out_specs=pl.BlockSpec((tm, D), lambda i, f, te, nu: (i, 0)),
        scratch_shapes=[pltpu.VMEM((tm, D), f32)])
    return pl.pallas_call(
        _moe_ffn_kernel,
        out_shape=jax.ShapeDtypeStruct((P, D), f32),
        grid_spec=grid_spec,
        compiler_params=_cparams(("arbitrary", "arbitrary"), 48),
        name="moe_ffn",
    )(tile_expert, n_used, hs, w1_all, w3_all, w2_all)


def _router_kernel(h_ref, wr_ref, o_ref):
    hh, hm = _split(h_ref[...], 2)
    wh, wm = _split(wr_ref[...], 2)
    logits = (jnp.dot(hh, wh, preferred_element_type=f32) + jnp.dot(hh, wm, preferred_element_type=f32)
              + jnp.dot(hm, wh, preferred_element_type=f32))
    lane = lax.broadcasted_iota(i32, logits.shape, 1)
    logits = jnp.where(lane < N_EXPERTS, logits, -jnp.inf)
    m1 = jnp.max(logits, axis=-1, keepdims=True)
    i1 = jnp.min(jnp.where(logits == m1, lane, LANE), axis=-1, keepdims=True)
    rest = jnp.where(lane == i1, -jnp.inf, logits)
    m2 = jnp.max(rest, axis=-1, keepdims=True)
    i2 = jnp.min(jnp.where(rest == m2, lane, LANE), axis=-1, keepdims=True)
    e2 = jnp.exp(m2 - m1)
    g1 = 1.0 / (1.0 + e2)
    g2 = e2 / (1.0 + e2)
    o_ref[...] = jnp.where(lane == 0, g1, jnp.where(lane == 1, g2, jnp.where(
        lane == 2, i1.astype(f32), jnp.where(lane == 3, i2.astype(f32), 0.0))))


def _router(h2, wr_pad_all, m):
    T, D = h2.shape
    tm = 512
    return pl.pallas_call(
        _router_kernel,
        out_shape=jax.ShapeDtypeStruct((T, LANE), f32),
        grid=(T // tm,),
        in_specs=[pl.BlockSpec((tm, D), lambda i: (i, 0)),
                  pl.BlockSpec((None, D, LANE), lambda i: (m, 0, 0))],
        out_specs=pl.BlockSpec((tm, LANE), lambda i: (i, 0)),
        compiler_params=_cparams(("arbitrary",), 32),
        name="moe_router",
    )(h2, wr_pad_all)


GATHER_ROWS = 256


def _row_copy(src_hbm, buf, sem, t, r):
    return pltpu.make_async_copy(src_hbm.at[pl.ds(t, 1), :], buf.at[pl.ds(r, 1), :], sem)


ROW_UNROLL = 8


def _issue_rows(idx_ref, src_hbm, buf, sem):
    def issue(r, carry):
        _row_copy(src_hbm, buf, sem, idx_ref[0, 0, r], r).start()
        return carry

    lax.fori_loop(0, buf.shape[0], issue, 0, unroll=ROW_UNROLL)


def _wait_rows(src_hbm, buf, sem):
    def wait(r, carry):
        _row_copy(src_hbm, buf, sem, 0, r).wait()
        return carry

    lax.fori_loop(0, buf.shape[0], wait, 0, unroll=ROW_UNROLL)


def _prefetched_gather(step_idx_refs, next_idx_refs, src_hbm, bufs, sems):
    i = pl.program_id(0)
    slot = lax.rem(i, 2)
    for idx_ref, buf, sem in zip(step_idx_refs, bufs, sems):
        @pl.when(i == 0)
        def _(idx_ref=idx_ref, buf=buf, sem=sem):
            _issue_rows(idx_ref, src_hbm, buf.at[0], sem.at[0])
    for idx_ref, buf, sem in zip(next_idx_refs, bufs, sems):
        @pl.when(i + 1 < pl.num_programs(0))
        def _(idx_ref=idx_ref, buf=buf, sem=sem):
            _issue_rows(idx_ref, src_hbm, buf.at[1 - slot], sem.at[1 - slot])
    for buf, sem in zip(bufs, sems):
        _wait_rows(src_hbm, buf.at[slot], sem.at[slot])
    return slot


def _gather_kernel(idx_ref, nxt_ref, src_hbm, o_ref, buf, sem):
    slot = _prefetched_gather([idx_ref], [nxt_ref], src_hbm, [buf], [sem])
    o_ref[...] = buf[slot].astype(o_ref.dtype)


def _idx_specs(n_tiles, rt):
    cur = pl.BlockSpec((1, 1, rt), lambda i: (i, 0, 0), memory_space=pltpu.SMEM)
    nxt = pl.BlockSpec((1, 1, rt), lambda i: (jnp.minimum(i + 1, n_tiles - 1), 0, 0),
                       memory_space=pltpu.SMEM)
    return cur, nxt


def _gather_rows(src, idx, out_dtype):
    P = idx.shape[0]
    D = src.shape[1]
    rt = GATHER_ROWS
    cur, nxt = _idx_specs(P // rt, rt)
    idx3 = idx.reshape(P // rt, 1, rt)
    return pl.pallas_call(
        _gather_kernel,
        out_shape=jax.ShapeDtypeStruct((P, D), out_dtype),
        grid=(P // rt,),
        in_specs=[cur, nxt, pl.BlockSpec(memory_space=pl.ANY)],
        out_specs=pl.BlockSpec((rt, D), lambda i: (i, 0)),
        scratch_shapes=[pltpu.VMEM((2, rt, D), f32), pltpu.SemaphoreType.DMA((2,))],
        compiler_params=_cparams(("arbitrary",), 32),
        name="moe_gather",
    )(idx3, idx3, src)


def _combine_kernel(p0_ref, p0n_ref, p1_ref, p1n_ref, ys_hbm, x_ref, gt_ref, mod_ref, nf_ref, o_ref,
                    b0, b1, sem0, sem1, *, final_norm):
    slot = _prefetched_gather([p0_ref, p1_ref], [p0n_ref, p1n_ref], ys_hbm, [b0, b1], [sem0, sem1])
    gt = gt_ref[...]
    y = gt[:, 0:1] * b0[slot] + gt[:, 1:2] * b1[slot]
    xn = x_ref[...] + mod_ref[0, 5:6, :] * y
    if final_norm:
        xn = _rms(xn) * nf_ref[...]
    o_ref[...] = xn


def _moe_combine(ys, pos0, pos1, x2, gates, mod, norm_final, l, L, final_norm):
    T, D = x2.shape
    rt = GATHER_ROWS
    per_b = L // rt
    cur, nxt = _idx_specs(T // rt, rt)
    p0 = pos0.reshape(T // rt, 1, rt)
    p1 = pos1.reshape(T // rt, 1, rt)
    return pl.pallas_call(
        functools.partial(_combine_kernel, final_norm=final_norm),
        out_shape=jax.ShapeDtypeStruct((T, D), f32),
        grid=(T // rt,),
        in_specs=[cur, nxt, cur, nxt, pl.BlockSpec(memory_space=pl.ANY),
                  pl.BlockSpec((rt, D), lambda i: (i, 0)),
                  pl.BlockSpec((rt, LANE), lambda i: (i, 0)),
                  pl.BlockSpec((None, 1, 6, D), _mod_spec(l, per_b, 1)),
                  pl.BlockSpec((1, D), lambda i: (0, 0))],
        out_specs=pl.BlockSpec((rt, D), lambda i: (i, 0)),
        scratch_shapes=[pltpu.VMEM((2, rt, D), f32), pltpu.VMEM((2, rt, D), f32),
                        pltpu.SemaphoreType.DMA((2,)), pltpu.SemaphoreType.DMA((2,))],
        compiler_params=_cparams(("arbitrary",), 40),
        name="moe_combine",
    )(p0, p0, p1, p1, ys, x2, gates, mod, norm_final)


def _moe_plan(route, T):
    tm = MOE_TM
    P = 2 * T + N_EXPERTS * tm
    e_flat = route[:, 2:4].astype(i32).reshape(-1)
    onehot = (e_flat[:, None] == jnp.arange(N_EXPERTS, dtype=i32)[None, :]).astype(i32)
    csum = jnp.cumsum(onehot, axis=0)
    counts = csum[-1]
    rank = jnp.sum(csum * onehot, axis=1) - 1
    tiles = (counts + tm - 1) // tm
    tile_end = jnp.cumsum(tiles)
    off = (tile_end - tiles) * tm
    pos = off[e_flat] + rank
    tok = jnp.arange(2 * T, dtype=i32) // 2
    src = jnp.zeros((P,), i32).at[pos].set(tok, unique_indices=True)
    n_used = tile_end[-1:]
    tidx = jnp.arange(P // tm, dtype=i32)
    te = jnp.sum((tidx[:, None] >= tile_end[None, :]).astype(i32), axis=1)
    te_last = jnp.sum((n_used - 1 >= tile_end).astype(i32))
    te = jnp.where(tidx < n_used, jnp.minimum(te, N_EXPERTS - 1), te_last).astype(i32)
    pos2 = pos.reshape(T, 2)
    return src, pos2[:, 0], pos2[:, 1], te, n_used.astype(i32)


def _permute_w_in(w_in):
    depth, d, _ = w_in.shape
    pad = jnp.zeros((depth, d, PROJ_W - OFF_SM - 48), w_in.dtype)
    parts = [w_in[:, :, 0:3072], w_in[:, :, 4376:5144], w_in[:, :, 3584:4352], w_in[:, :, 3072:3584],
             w_in[:, :, 5168:6704], w_in[:, :, 4352:4376], w_in[:, :, 5144:5168], pad]
    return jnp.concatenate(parts, axis=-1).astype(bf16)


def _small_rows(gdn_p, ssm_p):
    depth = gdn_p.shape[0]
    row = jnp.zeros((depth, LANE), f32)
    row = row.at[:, SM_GA:SM_GA + 2 * GDN_HEADS].set(gdn_p.reshape(depth, -1).astype(f32))
    row = row.at[:, SM_DT:SM_DT + 2 * SSM_HEADS].set(ssm_p.reshape(depth, -1).astype(f32))
    return row.reshape(depth, 1, LANE)


def kernel(x, c, ada_w, ada_b, norm_mix, norm_ffn, norm_final, w_in, conv_w, conv_b, gdn_A_log, gdn_dt_bias, gdn_norm, ssm_A_log, ssm_dt_bias, ssm_D, ssm_norm, na_rpb, w_out, ffn_w1, ffn_w3, ffn_w2, moe_router, moe_w1, moe_w3, moe_w2):
    bsz, L, D = x.shape
    depth = w_in.shape[0]
    T = bsz * L

    mod = _ada_mod(c, ada_w, ada_b).reshape(depth, bsz, 6, D)
    w_in_p = _permute_w_in(w_in)
    w_out_b = w_out.astype(bf16)
    alog_rows = _small_rows(gdn_A_log, ssm_A_log)
    dtb_rows = _small_rows(gdn_dt_bias, ssm_dt_bias)
    dskip_rows = jnp.repeat(ssm_D.astype(f32), SSM_HEADDIM, axis=-1).reshape(depth, 1, SSM_WIDTH)
    bias_tabs = _na_bias_tables(na_rpb)
    ffn_w1b, ffn_w3b, ffn_w2b = ffn_w1.astype(bf16), ffn_w3.astype(bf16), ffn_w2.astype(bf16)
    moe_w1b, moe_w3b, moe_w2b = moe_w1.astype(bf16), moe_w3.astype(bf16), moe_w2.astype(bf16)
    wr_pad = jnp.pad(moe_router.astype(f32), ((0, 0), (0, 0), (0, LANE - N_EXPERTS)))

    x2 = x.reshape(T, D)
    for l in range(depth):
        is_moe = l % 2 == 1
        proj = _inproj(x2, mod, norm_mix[l].reshape(1, D), w_in_p, l, L).reshape(bsz, L, PROJ_W)
        prep = _mix_prep(proj, alog_rows[l], dtb_rows[l])
        cb = conv_b[l].reshape(1, CONV_CH)
        ya = _gdn_mixer(proj, prep, conv_w[l], cb, gdn_norm[l].reshape(1, LANE))
        yb = _ssd_mixer(proj, prep, conv_w[l], cb, dskip_rows[l])
        yc = _na_mixer(proj, bias_tabs, l)
        x2, h2 = _outproj(ya.reshape(T, GDN_WIDTH), yb.reshape(T, SSM_WIDTH), yc.reshape(T, NA_WIDTH),
                          ssm_norm[l].reshape(1, SSM_WIDTH), w_out_b, x2, mod, norm_ffn[l].reshape(1, D),
                          l, L, f32 if is_moe else bf16)
        m = l // 2
        if not is_moe:
            x2 = _ffn_dense(h2, ffn_w1b, ffn_w3b, ffn_w2b, m, x2, mod, l, L)
        else:
            route = _router(h2, wr_pad, m)
            src, pos0, pos1, te, n_used = _moe_plan(route, T)
            hs = _gather_rows(h2, src, bf16)
            ys = _moe_ffn(hs, moe_w1b, moe_w3b, moe_w2b, m, te, n_used)
            x2 = _moe_combine(ys, pos0, pos1, x2, route, mod, norm_final.reshape(1, D), l, L,
                              final_norm=(l == depth - 1))
    if depth % 2 == 1:
        raise NotImplementedError("the final norm is fused into the last MoE combine")
    return x2.reshape(bsz, L, D)
```

```python
import functools
import math

import jax
import jax.numpy as jnp
import numpy as np
from jax import lax
from jax.experimental import pallas as pl
from jax.experimental.pallas import tpu as pltpu

f32 = jnp.float32
bf16 = jnp.bfloat16
i32 = jnp.int32

GRID_W = 64
GDN_HEADS = 6
GDN_DK = 128
GDN_WIDTH = 768
SSM_HEADS = 12
SSM_HEADDIM = 64
SSM_WIDTH = 768
SSM_GROUPS = 2
SSM_HPG = 6
SSM_STATE = 128
NA_HEADS = 8
NA_HEADDIM = 64
NA_WIDTH = 512
NA_KH = 8
NA_KW = 16
CONV_K = 5
CONV_CH = 3584
N_EXPERTS = 8
EPS = 1e-6

OFF_GQ, OFF_GK, OFF_GV, OFF_SX, OFF_SZ, OFF_GZ = 0, 768, 1536, 2304, 3072, 3840
OFF_SB, OFF_SC, OFF_NQ, OFF_NK, OFF_NV, OFF_SM = 4608, 4864, 5120, 5632, 6144, 6656
PROJ_W = 6912
SM_BETA, SM_GA, SM_DT = 0, 12, 24

LANE = 128
MIX_CHUNK = 128
CONV_HALO = 8
MIB = 1024 * 1024


def _cparams(sem, vmem_mib):
    return pltpu.CompilerParams(dimension_semantics=sem, vmem_limit_bytes=vmem_mib * MIB)


def _sigmoid(x):
    return 1.0 / (1.0 + jnp.exp(-x))


def _softplus(x):
    return jnp.maximum(x, 0.0) + jnp.log1p(jnp.exp(-jnp.abs(x)))


def _dot(a, b):
    return jnp.dot(a.astype(bf16), b.astype(bf16), preferred_element_type=f32)


def _dot_nt(a, b):
    return lax.dot_general(a.astype(bf16), b.astype(bf16), (((1,), (1,)), ((), ())),
                           preferred_element_type=f32)


def _split(x, terms):
    out = []
    for _ in range(terms - 1):
        piece = x.astype(bf16)
        out.append(piece)
        x = x - piece.astype(f32)
    out.append(x.astype(bf16))
    return out


def _sel_left(m01, x, terms=3):
    m = m01.astype(bf16)
    return sum(jnp.dot(m, t, preferred_element_type=f32) for t in _split(x, terms))


def _sel_right(x, m01, terms=2):
    m = m01.astype(bf16)
    return sum(jnp.dot(t, m, preferred_element_type=f32) for t in _split(x, terms))


def _rms(x):
    return x * lax.rsqrt(jnp.mean(x * x, axis=-1, keepdims=True) + EPS)


def _conv_silu(xp_ref, cw_ref, cb_ref, r0, rows, blk=0):
    lanes = slice(blk * LANE, (blk + 1) * LANE)
    acc = jnp.broadcast_to(cb_ref[:, lanes], (rows, LANE))
    for j in range(CONV_K):
        shift = CONV_HALO - CONV_K // 2 + j
        acc = acc + cw_ref[j:j + 1, lanes] * xp_ref[blk, pl.ds(r0 + shift, rows), :]
    return acc * _sigmoid(acc)


def _fill_padded(src_ref, dst_ref, L):
    for blk in range(dst_ref.shape[0]):
        dst_ref[blk, 0:CONV_HALO, :] = jnp.zeros((CONV_HALO, LANE), f32)
        dst_ref[blk, L + CONV_HALO:L + 2 * CONV_HALO, :] = jnp.zeros((CONV_HALO, LANE), f32)
        dst_ref[blk, CONV_HALO:L + CONV_HALO, :] = src_ref[0, :, blk * LANE:(blk + 1) * LANE]


def _pick_col(blk, idx):
    lane = lax.broadcasted_iota(i32, blk.shape, 1)
    return jnp.sum(jnp.where(lane == idx, blk, 0.0), axis=-1, keepdims=True)


def _ada_kernel(c_ref, w_ref, b_ref, o_ref):
    c = c_ref[...]
    cond = c * _sigmoid(c)
    o_ref[0] = _dot(cond, w_ref[0]) + b_ref[0]


def _ada_mod(c, ada_w, ada_b):
    depth, d, n = ada_w.shape
    bsz = c.shape[0]
    tn = 1536
    return pl.pallas_call(
        _ada_kernel,
        out_shape=jax.ShapeDtypeStruct((depth, bsz, n), f32),
        grid=(depth, n // tn),
        in_specs=[pl.BlockSpec((bsz, d), lambda l, j: (0, 0)),
                  pl.BlockSpec((1, d, tn), lambda l, j: (l, 0, j)),
                  pl.BlockSpec((1, 1, tn), lambda l, j: (l, 0, j))],
        out_specs=pl.BlockSpec((1, bsz, tn), lambda l, j: (l, 0, j)),
        compiler_params=_cparams(("arbitrary", "arbitrary"), 40),
        name="ada_mod",
    )(c, ada_w, ada_b.reshape(depth, 1, n))


NORM_ROWS = 128


def _norm_mod_rows(x_ref, nw, sc, sh, h_ref):
    def body(i, carry):
        r = pl.multiple_of(i * NORM_ROWS, NORM_ROWS)
        y = _rms(x_ref[pl.ds(r, NORM_ROWS), :]) * nw
        h_ref[pl.ds(r, NORM_ROWS), :] = (y * (1.0 + sc) + sh).astype(h_ref.dtype)
        return carry
    lax.fori_loop(0, x_ref.shape[0] // NORM_ROWS, body, 0)


def _inproj_kernel(x_ref, mod_ref, nw_ref, w_ref, o_ref, h_ref):
    @pl.when(pl.program_id(1) == 0)
    def _():
        _norm_mod_rows(x_ref, nw_ref[...], mod_ref[0, 1:2, :], mod_ref[0, 0:1, :], h_ref)
    o_ref[...] = jnp.dot(h_ref[...], w_ref[...], preferred_element_type=f32)


def _mod_spec(l, per_b, nargs):
    if nargs == 1:
        return lambda i: (l, i // per_b, 0, 0)
    return lambda i, j: (l, i // per_b, 0, 0)


def _inproj(x2, mod, nw, w_all, l, L):
    T, D = x2.shape
    NP = w_all.shape[2]
    tm, tn = min(1024, L), 768
    per_b = L // tm
    return pl.pallas_call(
        _inproj_kernel,
        out_shape=jax.ShapeDtypeStruct((T, NP), f32),
        grid=(T // tm, NP // tn),
        in_specs=[pl.BlockSpec((tm, D), lambda i, j: (i, 0)),
                  pl.BlockSpec((None, 1, 6, D), _mod_spec(l, per_b, 2)),
                  pl.BlockSpec((1, D), lambda i, j: (0, 0)),
                  pl.BlockSpec((None, D, tn), lambda i, j: (l, 0, j))],
        out_specs=pl.BlockSpec((tm, tn), lambda i, j: (i, j)),
        scratch_shapes=[pltpu.VMEM((tm, D), bf16)],
        compiler_params=_cparams(("arbitrary", "arbitrary"), 44),
        name="inproj",
    )(x2, mod, nw, w_all)


def _prep_kernel(sm_ref, alog_ref, dtb_ref, act_ref, cum_ref, cumt_ref, tot_ref):
    L = sm_ref.shape[1]
    C = MIX_CHUNK
    ri = lax.broadcasted_iota(i32, (C, C), 0)
    ci = lax.broadcasted_iota(i32, (C, C), 1)
    trilf = jnp.where(ri >= ci, 1.0, 0.0)
    lane = lax.broadcasted_iota(i32, (C, LANE), 1)
    gdn_rev = jnp.logical_and(lane >= SM_GA + GDN_HEADS, lane < SM_DT)
    backward = jnp.logical_or(gdn_rev, lane >= SM_DT + SSM_HEADS)
    neg_a = -jnp.exp(alog_ref[...])
    dtb = dtb_ref[...]

    def body(c, carry):
        r0 = pl.multiple_of(c * C, C)
        sm = sm_ref[0, pl.ds(r0, C), :]
        sp = _softplus(sm + dtb)
        act_ref[0, pl.ds(r0, C), :] = jnp.where(lane < SM_GA, _sigmoid(sm), sp)
        a = neg_a * sp
        cum = _sel_left(trilf, a)
        tot = cum[C - 1:C, :]
        cd = jnp.where(backward, tot - cum + a, cum)
        cum_ref[0, pl.ds(r0, C), :] = cd
        cumt_ref[0, c] = cd.T
        tot_ref[0, c] = jnp.broadcast_to(tot, (8, LANE))
        return carry

    lax.fori_loop(0, L // C, body, 0)


def _mix_prep(proj, alog_row, dtb_row):
    bsz, L, _ = proj.shape
    C = MIX_CHUNK
    nc = L // C
    row = pl.BlockSpec((1, LANE), lambda b: (0, 0))
    tok = pl.BlockSpec((1, L, LANE), lambda b: (b, 0, 0))
    return pl.pallas_call(
        _prep_kernel,
        out_shape=(jax.ShapeDtypeStruct((bsz, L, LANE), f32), jax.ShapeDtypeStruct((bsz, L, LANE), f32),
                   jax.ShapeDtypeStruct((bsz, nc, LANE, C), f32), jax.ShapeDtypeStruct((bsz, nc, 8, LANE), f32)),
        grid=(bsz,),
        in_specs=[pl.BlockSpec((1, L, LANE), lambda b: (b, 0, OFF_SM // LANE)), row, row],
        out_specs=(tok, tok, pl.BlockSpec((1, nc, LANE, C), lambda b: (b, 0, 0, 0)),
                   pl.BlockSpec((1, nc, 8, LANE), lambda b: (b, 0, 0, 0))),
        compiler_params=_cparams(("arbitrary",), 32),
        name="mix_prep",
    )(proj, alog_row, dtb_row)


GDN_GROUP = 8
INV_BASE = 16


def _unit_tri_inverses(a_mats):
    C = a_mats[0].shape[0]
    ri = lax.broadcasted_iota(i32, (C, C), 0)
    ci = lax.broadcasted_iota(i32, (C, C), 1)
    eye = jnp.where(ri == ci, 1.0, 0.0)

    def same(b):
        s = int(math.log2(b))
        return lax.shift_right_logical(ri, s) == lax.shift_right_logical(ci, s)

    ps = [jnp.where(same(INV_BASE), a, 0.0) for a in a_mats]
    ts = [eye - p for p in ps]
    for _ in range(int(math.log2(INV_BASE)) - 1):
        ps = [_dot(p, p) for p in ps]
        ts = [t + _dot(t, p) for t, p in zip(ts, ps)]
    b = INV_BASE
    while b < C:
        off = jnp.logical_and(same(2 * b), jnp.logical_not(same(b)))
        us = [_dot(jnp.where(off, a, 0.0), t) for a, t in zip(a_mats, ts)]
        ts = [t - _dot(t, u) for t, u in zip(ts, us)]
        b *= 2
    return ts


def _gdn_kernel(q_ref, k_ref, v_ref, z_ref, act_ref, cum_ref, cumt_ref, tot_ref,
                cwq, cwk, cwv, cbq, cbk, cbv, nw_ref, o_ref,
                qp, kp, vp, kmat_s, nmat_s, qeff_s, gl_s, oacc):
    h = pl.program_id(1)
    L = q_ref.shape[1]
    C = MIX_CHUNK
    nc = L // C
    group = math.gcd(GDN_GROUP, nc)
    _fill_padded(q_ref, qp, L)
    _fill_padded(k_ref, kp, L)
    _fill_padded(v_ref, vp, L)
    oacc[...] = jnp.zeros(oacc.shape, f32)

    def phase_a(gi, carry):
        ri = lax.broadcasted_iota(i32, (C, C), 0)
        ci = lax.broadcasted_iota(i32, (C, C), 1)
        chains = []
        for j in range(group):
            c = gi * group + j
            r0 = pl.multiple_of(c * C, C)
            q = _conv_silu(qp, cwq, cbq, r0, C)
            k = _conv_silu(kp, cwk, cbk, r0, C)
            v = _conv_silu(vp, cwv, cbv, r0, C)
            qn = q * lax.rsqrt(jnp.sum(q * q, axis=-1, keepdims=True) + EPS) * (GDN_DK ** -0.5)
            kn = k * lax.rsqrt(jnp.sum(k * k, axis=-1, keepdims=True) + EPS)
            kk = _dot_nt(kn, kn)
            qk = _dot_nt(qn, kn)
            act = act_ref[0, pl.ds(r0, C), :]
            cum = cum_ref[0, pl.ds(r0, C), :]
            for d in (0, 1):
                gidx = SM_GA + d * GDN_HEADS + h
                beta = _pick_col(act, SM_BETA + d * GDN_HEADS + h)
                gcol = _pick_col(cum, gidx)
                grow = cumt_ref[0, c, pl.ds(gidx, 1), :]
                gl = _pick_col(tot_ref[0, c, 0:1, :], gidx)
                mask = (ri >= ci) if d == 0 else (ri <= ci)
                strict = (ri > ci) if d == 0 else (ri < ci)
                dec = jnp.exp(jnp.where(mask, gcol - grow, -jnp.inf))
                a_mat = jnp.where(strict, beta * kk * dec, 0.0)
                chains.append((c, r0, d, a_mat, dec, beta, gcol, gl, qn, kn, v, qk))
        t_invs = _unit_tri_inverses([ch[3] for ch in chains])
        for (c, r0, d, _, dec, beta, gcol, gl, qn, kn, v, qk), t_inv in zip(chains, t_invs):
            eg = jnp.exp(gcol)
            sol = _dot(t_inv, jnp.concatenate([v * beta, kn * (beta * eg)], axis=1))
            kd = kn * jnp.exp(gl - gcol)
            big = _dot(jnp.concatenate([kd.T, qk * dec], axis=0), sol)
            nmat_s[d, c] = big[0:GDN_DK, 0:LANE]
            kmat_s[d, c] = big[0:GDN_DK, LANE:].astype(bf16)
            qeff_s[d, pl.ds(r0, C), :] = (qn * eg - big[GDN_DK:, LANE:]).astype(bf16)
            oacc[pl.ds(r0, C), :] += big[GDN_DK:, 0:LANE]
            gl_s[d, c] = jnp.broadcast_to(jnp.exp(gl), (8, LANE))
        return carry

    lax.fori_loop(0, nc // group, phase_a, 0)

    def phase_b(i, carry):
        out = []
        for d in (0, 1):
            s = carry[d]
            c = i if d == 0 else nc - 1 - i
            r0 = pl.multiple_of(c * C, C)
            sb = s.astype(bf16)
            oacc[pl.ds(r0, C), :] += jnp.dot(qeff_s[d, pl.ds(r0, C), :], sb, preferred_element_type=f32)
            out.append(s * gl_s[d, c][0:1, :] - jnp.dot(kmat_s[d, c], sb, preferred_element_type=f32)
                       + nmat_s[d, c])
        return tuple(out)

    s0 = jnp.zeros((GDN_DK, LANE), f32)
    lax.fori_loop(0, nc, phase_b, (s0, s0))

    def fin(c, carry):
        r0 = pl.multiple_of(c * C, C)
        y = _rms(oacc[pl.ds(r0, C), :]) * nw_ref[...]
        z = z_ref[0, pl.ds(r0, C), :]
        o_ref[0, pl.ds(r0, C), :] = (y * (z * _sigmoid(z))).astype(o_ref.dtype)
        return carry

    lax.fori_loop(0, nc, fin, 0)


def _prep_specs(L, nc, C):
    tok = pl.BlockSpec((1, L, LANE), lambda b, h: (b, 0, 0))
    return [tok, tok, pl.BlockSpec((1, nc, LANE, C), lambda b, h: (b, 0, 0, 0)),
            pl.BlockSpec((1, nc, 8, LANE), lambda b, h: (b, 0, 0, 0))]


def _gdn_mixer(proj, prep, conv_w, conv_b, gdn_norm):
    bsz, L, _ = proj.shape
    C = MIX_CHUNK
    nc = L // C
    blk = lambda off: pl.BlockSpec((1, L, LANE), lambda b, h, o=off // LANE: (b, 0, o + h))
    cwb = lambda off: pl.BlockSpec((CONV_K, LANE), lambda b, h, o=off // LANE: (0, o + h))
    cbb = lambda off: pl.BlockSpec((1, LANE), lambda b, h, o=off // LANE: (0, o + h))
    return pl.pallas_call(
        _gdn_kernel,
        out_shape=jax.ShapeDtypeStruct((bsz, L, GDN_WIDTH), bf16),
        grid=(bsz, GDN_HEADS),
        in_specs=[blk(OFF_GQ), blk(OFF_GK), blk(OFF_GV), blk(OFF_GZ)] + _prep_specs(L, nc, C) + [
            cwb(0), cwb(768), cwb(1536), cbb(0), cbb(768), cbb(1536),
            pl.BlockSpec((1, LANE), lambda b, h: (0, 0))],
        out_specs=pl.BlockSpec((1, L, LANE), lambda b, h: (b, 0, h)),
        scratch_shapes=[pltpu.VMEM((1, L + 2 * CONV_HALO, LANE), f32)] * 3 + [
            pltpu.VMEM((2, nc, GDN_DK, LANE), bf16),
            pltpu.VMEM((2, nc, GDN_DK, LANE), f32),
            pltpu.VMEM((2, L, LANE), bf16),
            pltpu.VMEM((2, nc, 8, LANE), f32),
            pltpu.VMEM((L, LANE), f32)],
        compiler_params=_cparams(("arbitrary", "arbitrary"), 40),
        name="gdn_mixer",
    )(proj, proj, proj, proj, *prep, conv_w, conv_w, conv_w, conv_b, conv_b, conv_b, gdn_norm)


SSD_W = SSM_HPG * SSM_HEADDIM


def _ssd_kernel(x_ref, z_ref, b_ref, c_ref, act_ref, cum_ref, cumt_ref, tot_ref,
                cwx, cwb, cwc, cbx, cbb, cbc, dsk_ref, o_ref, xp, bp, cp, xc, bc, cc, yacc):
    g = pl.program_id(1)
    L = x_ref.shape[1]
    C = MIX_CHUNK
    nc = L // C
    _fill_padded(x_ref, xp, L)
    _fill_padded(b_ref, bp, L)
    _fill_padded(c_ref, cp, L)
    yacc[...] = jnp.zeros(yacc.shape, f32)

    def conv_all(c, carry):
        r0 = pl.multiple_of(c * C, C)
        for blk in range(SSD_W // LANE):
            xc[pl.ds(r0, C), blk * LANE:(blk + 1) * LANE] = _conv_silu(xp, cwx, cbx, r0, C, blk)
        bc[pl.ds(r0, C), :] = _conv_silu(bp, cwb, cbb, r0, C)
        cc[pl.ds(r0, C), :] = _conv_silu(cp, cwc, cbc, r0, C)
        return carry

    lax.fori_loop(0, nc, conv_all, 0)

    def step(i, carry):
        ri = lax.broadcasted_iota(i32, (C, C), 0)
        ci = lax.broadcasted_iota(i32, (C, C), 1)
        er = lax.broadcasted_iota(i32, (LANE, SSD_W), 0)
        el = lax.shift_right_logical(lax.broadcasted_iota(i32, (LANE, SSD_W), 1),
                                     int(math.log2(SSM_HEADDIM)))
        half = lax.broadcasted_iota(i32, (C, LANE), 1) < SSM_HEADDIM
        out = []
        for d in (0, 1):
            ht = carry[d]
            c = i if d == 0 else nc - 1 - i
            r0 = pl.multiple_of(c * C, C)
            base = SM_DT + d * SSM_HEADS + g * SSM_HPG
            expand = jnp.where(er == base + el, 1.0, 0.0)
            cum = cum_ref[0, pl.ds(r0, C), :]
            g6 = _sel_right(cum, expand)
            dt6 = _sel_right(act_ref[0, pl.ds(r0, C), :], expand)
            tot6 = _sel_right(tot_ref[0, c], expand)[0:1, :]
            xdt = xc[pl.ds(r0, C), :] * dt6
            bm = bc[pl.ds(r0, C), :]
            cm = cc[pl.ds(r0, C), :]
            cb = _dot_nt(cm, bm)
            mask = (ri >= ci) if d == 0 else (ri <= ci)
            pieces = []
            for pr in range(SSM_HPG // 2):
                xpair = xdt[:, pr * LANE:(pr + 1) * LANE].astype(bf16)
                ys = []
                for e in (0, 1):
                    gidx = base + 2 * pr + e
                    diff = _pick_col(cum, gidx) - cumt_ref[0, c, pl.ds(gidx, 1), :]
                    lm = jnp.exp(jnp.where(mask, diff, -jnp.inf))
                    ys.append(jnp.dot((cb * lm).astype(bf16), xpair, preferred_element_type=f32))
                pieces.append(jnp.where(half, ys[0], ys[1]))
            y_diag = jnp.concatenate(pieces, axis=1)
            y_off = _dot(cm, ht) * jnp.exp(g6)
            yacc[pl.ds(r0, C), :] += y_diag + y_off
            st = _dot(bm.T, xdt * jnp.exp(tot6 - g6))
            out.append(ht * jnp.exp(tot6) + st)
        return tuple(out)

    h0 = jnp.zeros((SSM_STATE, SSD_W), f32)
    lax.fori_loop(0, nc, step, (h0, h0), unroll=2)

    def fin(c, carry):
        r0 = pl.multiple_of(c * C, C)
        y = yacc[pl.ds(r0, C), :] + xc[pl.ds(r0, C), :] * dsk_ref[...]
        z = z_ref[0, pl.ds(r0, C), :]
        o_ref[0, pl.ds(r0, C), :] = y * (z * _sigmoid(z))
        return carry

    lax.fori_loop(0, nc, fin, 0)


def _ssd_mixer(proj, prep, conv_w, conv_b, dskip_row):
    bsz, L, _ = proj.shape
    C = MIX_CHUNK
    nc = L // C
    wide = lambda off: pl.BlockSpec((1, L, SSD_W), lambda b, g, o=off // SSD_W: (b, 0, o + g))
    nar = lambda off: pl.BlockSpec((1, L, LANE), lambda b, g, o=off // LANE: (b, 0, o + g))
    return pl.pallas_call(
        _ssd_kernel,
        out_shape=jax.ShapeDtypeStruct((bsz, L, SSM_WIDTH), f32),
        grid=(bsz, SSM_GROUPS),
        in_specs=[wide(OFF_SX), wide(OFF_SZ), nar(OFF_SB), nar(OFF_SC)] + _prep_specs(L, nc, C) + [
            pl.BlockSpec((CONV_K, SSD_W), lambda b, g: (0, 2304 // SSD_W + g)),
            pl.BlockSpec((CONV_K, LANE), lambda b, g: (0, 3072 // LANE + g)),
            pl.BlockSpec((CONV_K, LANE), lambda b, g: (0, 3328 // LANE + g)),
            pl.BlockSpec((1, SSD_W), lambda b, g: (0, 2304 // SSD_W + g)),
            pl.BlockSpec((1, LANE), lambda b, g: (0, 3072 // LANE + g)),
            pl.BlockSpec((1, LANE), lambda b, g: (0, 3328 // LANE + g)),
            pl.BlockSpec((1, SSD_W), lambda b, g: (0, g))],
        out_specs=pl.BlockSpec((1, L, SSD_W), lambda b, g: (b, 0, g)),
        scratch_shapes=[pltpu.VMEM((SSD_W // LANE, L + 2 * CONV_HALO, LANE), f32),
                        pltpu.VMEM((1, L + 2 * CONV_HALO, LANE), f32),
                        pltpu.VMEM((1, L + 2 * CONV_HALO, LANE), f32),
                        pltpu.VMEM((L, SSD_W), f32),
                        pltpu.VMEM((L, LANE), f32),
                        pltpu.VMEM((L, LANE), f32),
                        pltpu.VMEM((L, SSD_W), f32)],
        compiler_params=_cparams(("arbitrary", "arbitrary"), 48),
        name="ssd_mixer",
    )(proj, proj, proj, proj, *prep, conv_w, conv_w, conv_w, conv_b, conv_b, conv_b, dskip_row)


NA_WIN = NA_KH * GRID_W
NA_UNROLL = 4


def _na_kernel(q_ref, k_ref, v_ref, bias_ref, o_ref, kb, vb, s_scr, p_scr, den_scr):
    L = q_ref.shape[1]
    rows = L // GRID_W
    kb[...] = k_ref[0].astype(bf16)
    vb[...] = v_ref[0].astype(bf16)

    def body(g, carry):
        low = lax.broadcasted_iota(i32, (GRID_W, LANE), 1) < NA_HEADDIM
        wins = []
        for j in range(NA_UNROLL):
            r = g * NA_UNROLL + j
            rs = jnp.clip(r - NA_KH // 2, 0, rows - NA_KH)
            off = r - rs
            q0 = pl.multiple_of(r * GRID_W, GRID_W)
            w0 = pl.multiple_of(rs * GRID_W, GRID_W)
            q = q_ref[0, pl.ds(q0, GRID_W), :] * (NA_HEADDIM ** -0.5)
            kw = kb[pl.ds(w0, NA_WIN), :]
            for e in (0, 1):
                qm = jnp.where(low if e == 0 else jnp.logical_not(low), q, 0.0)
                s_scr[j, e] = _dot_nt(qm, kw) + bias_ref[e, off]
            wins.append((q0, w0))
        for j in range(NA_UNROLL):
            for e in (0, 1):
                s = s_scr[j, e]
                p = jnp.exp(s - jnp.max(s, axis=-1, keepdims=True))
                den_scr[j, e] = jnp.broadcast_to(jnp.sum(p, axis=-1, keepdims=True), (GRID_W, LANE))
                p_scr[j, e] = p.astype(bf16)
        for j, (q0, w0) in enumerate(wins):
            vw = vb[pl.ds(w0, NA_WIN), :]
            outs = [jnp.dot(p_scr[j, e], vw, preferred_element_type=f32) / den_scr[j, e] for e in (0, 1)]
            o_ref[0, pl.ds(q0, GRID_W), :] = jnp.where(low, outs[0], outs[1]).astype(o_ref.dtype)
        return carry

    lax.fori_loop(0, rows // NA_UNROLL, body, 0)


def _na_bias_tables(rpb):
    depth = rpb.shape[0]
    o = np.arange(NA_KH)[:, None]
    i = np.arange(NA_KH)[None, :]
    dr_idx = (i - o) + NA_KH - 1
    cols = np.arange(GRID_W)
    col_start = np.clip(cols - NA_KW // 2, 0, GRID_W - NA_KW)
    in_win = (cols[None, :] >= col_start[:, None]) & (cols[None, :] < col_start[:, None] + NA_KW)
    dc_idx = np.clip(cols[None, :] - cols[:, None], -(NA_KW - 1), NA_KW - 1) + NA_KW - 1
    oh_r = (dr_idx[..., None] == np.arange(2 * NA_KH - 1)).astype(np.float32)
    oh_c = (dc_idx[..., None] == np.arange(2 * NA_KW - 1)).astype(np.float32)
    hi = lax.Precision.HIGHEST
    t = jnp.einsum('oir,lhrc->lhoic', oh_r, rpb.astype(f32), precision=hi)
    bias = jnp.einsum('lhoic,qkc->lhoqik', t, oh_c, precision=hi)
    bias = jnp.where(in_win[None, None, None, :, None, :], bias, -jnp.inf)
    return bias.reshape(depth, NA_HEADS, NA_KH, GRID_W, NA_WIN)


def _na_mixer(proj, bias_tabs, l):
    bsz, L, _ = proj.shape
    assert L // GRID_W >= NA_KH and (L // GRID_W) % NA_UNROLL == 0
    blk = lambda off: pl.BlockSpec((1, L, LANE), lambda b, p, o=off // LANE: (b, 0, o + p))
    return pl.pallas_call(
        _na_kernel,
        out_shape=jax.ShapeDtypeStruct((bsz, L, NA_WIDTH), bf16),
        grid=(bsz, NA_HEADS // 2),
        in_specs=[blk(OFF_NQ), blk(OFF_NK), blk(OFF_NV),
                  pl.BlockSpec((None, 2, NA_KH, GRID_W, NA_WIN), lambda b, p: (l, p, 0, 0, 0))],
        out_specs=pl.BlockSpec((1, L, LANE), lambda b, p: (b, 0, p)),
        scratch_shapes=[pltpu.VMEM((L, LANE), bf16), pltpu.VMEM((L, LANE), bf16),
                        pltpu.VMEM((NA_UNROLL, 2, GRID_W, NA_WIN), f32),
                        pltpu.VMEM((NA_UNROLL, 2, GRID_W, NA_WIN), bf16),
                        pltpu.VMEM((NA_UNROLL, 2, GRID_W, LANE), f32)],
        compiler_params=_cparams(("arbitrary", "arbitrary"), 32),
        name="na_mixer",
    )(proj, proj, proj, bias_tabs)


def _outproj_kernel(ya_ref, yb_ref, yc_ref, sn_ref, wa_ref, wb_ref, wc_ref, x_ref, mod_ref, nw_ref,
                    xo_ref, h2_ref, ybn):
    _norm_mod_rows(yb_ref, sn_ref[...], 0.0, 0.0, ybn)
    y = (jnp.dot(ya_ref[...], wa_ref[...], preferred_element_type=f32)
         + jnp.dot(ybn[...], wb_ref[...], preferred_element_type=f32)
         + jnp.dot(yc_ref[...], wc_ref[...], preferred_element_type=f32))
    xo_ref[...] = x_ref[...] + mod_ref[0, 2:3, :] * y
    _norm_mod_rows(xo_ref, nw_ref[...], mod_ref[0, 4:5, :], mod_ref[0, 3:4, :], h2_ref)


def _outproj(ya, yb, yc, ssm_norm, w_out_all, x2, mod, nw, l, L, h2_dtype):
    T, D = x2.shape
    tm = 512
    per_b = L // tm
    const = lambda shape: pl.BlockSpec(shape, lambda i: (0, 0), pipeline_mode=pl.Buffered(1))
    wspec = lambda rows_, blk_idx: pl.BlockSpec((None, rows_, D), lambda i: (l, blk_idx, 0),
                                                pipeline_mode=pl.Buffered(1))
    rows = lambda w: pl.BlockSpec((tm, w), lambda i: (i, 0))
    nb_off = (GDN_WIDTH + SSM_WIDTH) // NA_WIDTH
    return pl.pallas_call(
        _outproj_kernel,
        out_shape=(jax.ShapeDtypeStruct((T, D), f32), jax.ShapeDtypeStruct((T, D), h2_dtype)),
        grid=(T // tm,),
        in_specs=[rows(GDN_WIDTH), rows(SSM_WIDTH), rows(NA_WIDTH), const((1, SSM_WIDTH)),
                  wspec(GDN_WIDTH, 0), wspec(SSM_WIDTH, 1), wspec(NA_WIDTH, nb_off),
                  rows(D), pl.BlockSpec((None, 1, 6, D), _mod_spec(l, per_b, 1)), const((1, D))],
        out_specs=(rows(D), rows(D)),
        scratch_shapes=[pltpu.VMEM((tm, SSM_WIDTH), bf16)],
        compiler_params=_cparams(("arbitrary",), 48),
        name="outproj",
    )(ya, yb, yc, ssm_norm, w_out_all, w_out_all, w_out_all, x2, mod, nw)


def _swiglu_acc(h_ref, w1_ref, w3_ref, w2_ref, acc_ref):
    h = h_ref[...]
    a = jnp.dot(h, w1_ref[...].astype(bf16), preferred_element_type=f32)
    b = jnp.dot(h, w3_ref[...].astype(bf16), preferred_element_type=f32)
    u = (a * _sigmoid(a) * b).astype(bf16)
    acc_ref[...] += jnp.dot(u, w2_ref[...].astype(bf16), preferred_element_type=f32)


def _ffn_kernel(h_ref, w1_ref, w3_ref, w2_ref, x_ref, mod_ref, o_ref):
    f = pl.program_id(1)

    @pl.when(f == 0)
    def _():
        o_ref[...] = jnp.zeros(o_ref.shape, f32)

    _swiglu_acc(h_ref, w1_ref, w3_ref, w2_ref, o_ref)

    @pl.when(f == pl.num_programs(1) - 1)
    def _():
        o_ref[...] = x_ref[...] + mod_ref[0, 5:6, :] * o_ref[...]


FFN_TM = 1024


def _ffn_dense(h2, w1_all, w3_all, w2_all, m, x2, mod, l, L):
    T, D = x2.shape
    F = w1_all.shape[2]
    tm, tf = min(FFN_TM, L), 512
    per_b = L // tm
    return pl.pallas_call(
        _ffn_kernel,
        out_shape=jax.ShapeDtypeStruct((T, D), f32),
        grid=(T // tm, F // tf),
        in_specs=[pl.BlockSpec((tm, D), lambda i, f: (i, 0)),
                  pl.BlockSpec((None, D, tf), lambda i, f: (m, 0, f)),
                  pl.BlockSpec((None, D, tf), lambda i, f: (m, 0, f)),
                  pl.BlockSpec((None, tf, D), lambda i, f: (m, f, 0)),
                  pl.BlockSpec((tm, D), lambda i, f: (i, 0)),
                  pl.BlockSpec((None, 1, 6, D), _mod_spec(l, per_b, 2))],
        out_specs=pl.BlockSpec((tm, D), lambda i, f: (i, 0)),
        compiler_params=_cparams(("arbitrary", "arbitrary"), 60),
        name="ffn_dense",
    )(h2, w1_all, w3_all, w2_all, x2, mod)


MOE_TM = FFN_TM


def _moe_ffn_kernel(te_ref, nu_ref, h_ref, w1_ref, w3_ref, w2_ref, o_ref):
    i = pl.program_id(0)
    f = pl.program_id(1)

    @pl.when(f == 0)
    def _():
        o_ref[...] = jnp.zeros(o_ref.shape, f32)

    @pl.when(i < nu_ref[0])
    def _():
        _swiglu_acc(h_ref, w1_ref, w3_ref, w2_ref, o_ref)


def _moe_ffn(hs, w1_all, w3_all, w2_all, m, tile_expert, n_used):
    P, D = hs.shape
    F = w1_all.shape[3]
    tm, tf = MOE_TM, 512
    nf = F // tf

    def fe(i, f, nu):
        return jnp.where(i < nu[0], f, nf - 1)

    grid_spec = pltpu.PrefetchScalarGridSpec(
        num_scalar_prefetch=2,
        grid=(P // tm, nf),
        in_specs=[pl.BlockSpec((tm, D), lambda i, f, te, nu: (jnp.minimum(i, nu[0] - 1), 0)),
                  pl.BlockSpec((None, None, D, tf), lambda i, f, te, nu: (m, te[i], 0, fe(i, f, nu))),
                  pl.BlockSpec((None, None, D, tf), lambda i, f, te, nu: (m, te[i], 0, fe(i, f, nu))),
                  pl.BlockSpec((None, None, tf, D), lambda i, f, te, nu: (m, te[i], fe(i, f, nu), 0))],
        out_specs=pl.BlockSpec((tm, D), lambda i, f, te, nu: (i, 0)))
    return pl.pallas_call(
        _moe_ffn_kernel,
        out_shape=jax.ShapeDtypeStruct((P, D), f32),
        grid_spec=grid_spec,
        compiler_params=_cparams(("arbitrary", "arbitrary"), 60),
        name="moe_ffn",
    )(tile_expert, n_used, hs, w1_all, w3_all, w2_all)


def _router_kernel(h_ref, wr_ref, o_ref):
    hh, hm = _split(h_ref[...], 2)
    wh, wm = _split(wr_ref[...], 2)
    logits = (jnp.dot(hh, wh, preferred_element_type=f32) + jnp.dot(hh, wm, preferred_element_type=f32)
              + jnp.dot(hm, wh, preferred_element_type=f32))
    lane = lax.broadcasted_iota(i32, logits.shape, 1)
    logits = jnp.where(lane < N_EXPERTS, logits, -jnp.inf)
    m1 = jnp.max(logits, axis=-1, keepdims=True)
    i1 = jnp.min(jnp.where(logits == m1, lane, LANE), axis=-1, keepdims=True)
    rest = jnp.where(lane == i1, -jnp.inf, logits)
    m2 = jnp.max(rest, axis=-1, keepdims=True)
    i2 = jnp.min(jnp.where(rest == m2, lane, LANE), axis=-1, keepdims=True)
    e2 = jnp.exp(m2 - m1)
    g1 = 1.0 / (1.0 + e2)
    g2 = e2 / (1.0 + e2)
    o_ref[...] = jnp.where(lane == 0, g1, jnp.where(lane == 1, g2, jnp.where(
        lane == 2, i1.astype(f32), jnp.where(lane == 3, i2.astype(f32), 0.0))))


def _router(h2, wr_pad_all, m):
    T, D = h2.shape
    tm = 512
    return pl.pallas_call(
        _router_kernel,
        out_shape=jax.ShapeDtypeStruct((T, LANE), f32),
        grid=(T // tm,),
        in_specs=[pl.BlockSpec((tm, D), lambda i: (i, 0)),
                  pl.BlockSpec((None, D, LANE), lambda i: (m, 0, 0))],
        out_specs=pl.BlockSpec((tm, LANE), lambda i: (i, 0)),
        compiler_params=_cparams(("arbitrary",), 32),
        name="moe_router",
    )(h2, wr_pad_all)


GATHER_ROWS = 256


def _row_copy(src_hbm, buf, sem, t, r):
    return pltpu.make_async_copy(src_hbm.at[pl.ds(t, 1), :], buf.at[pl.ds(r, 1), :], sem)


ROW_UNROLL = 8


def _issue_rows(idx_ref, src_hbm, buf, sem):
    def issue(r, carry):
        _row_copy(src_hbm, buf, sem, idx_ref[0, 0, r], r).start()
        return carry

    lax.fori_loop(0, buf.shape[0], issue, 0, unroll=ROW_UNROLL)


def _wait_rows(src_hbm, buf, sem):
    def wait(r, carry):
        _row_copy(src_hbm, buf, sem, 0, r).wait()
        return carry

    lax.fori_loop(0, buf.shape[0], wait, 0, unroll=ROW_UNROLL)


def _prefetched_gather(step_idx_refs, next_idx_refs, src_hbm, bufs, sems):
    i = pl.program_id(0)
    slot = lax.rem(i, 2)
    for idx_ref, buf, sem in zip(step_idx_refs, bufs, sems):
        @pl.when(i == 0)
        def _(idx_ref=idx_ref, buf=buf, sem=sem):
            _issue_rows(idx_ref, src_hbm, buf.at[0], sem.at[0])
    for idx_ref, buf, sem in zip(next_idx_refs, bufs, sems):
        @pl.when(i + 1 < pl.num_programs(0))
        def _(idx_ref=idx_ref, buf=buf, sem=sem):
            _issue_rows(idx_ref, src_hbm, buf.at[1 - slot], sem.at[1 - slot])
    for buf, sem in zip(bufs, sems):
        _wait_rows(src_hbm, buf.at[slot], sem.at[slot])
    return slot


def _gather_kernel(idx_ref, nxt_ref, src_hbm, o_ref, buf, sem):
    slot = _prefetched_gather([idx_ref], [nxt_ref], src_hbm, [buf], [sem])
    o_ref[...] = buf[slot].astype(o_ref.dtype)


def _idx_specs(n_tiles, rt):
    cur = pl.BlockSpec((1, 1, rt), lambda i: (i, 0, 0), memory_space=pltpu.SMEM)
    nxt = pl.BlockSpec((1, 1, rt), lambda i: (jnp.minimum(i + 1, n_tiles - 1), 0, 0),
                       memory_space=pltpu.SMEM)
    return cur, nxt


def _gather_rows(src, idx, out_dtype):
    P = idx.shape[0]
    D = src.shape[1]
    rt = GATHER_ROWS
    cur, nxt = _idx_specs(P // rt, rt)
    idx3 = idx.reshape(P // rt, 1, rt)
    return pl.pallas_call(
        _gather_kernel,
        out_shape=jax.ShapeDtypeStruct((P, D), out_dtype),
        grid=(P // rt,),
        in_specs=[cur, nxt, pl.BlockSpec(memory_space=pl.ANY)],
        out_specs=pl.BlockSpec((rt, D), lambda i: (i, 0)),
        scratch_shapes=[pltpu.VMEM((2, rt, D), f32), pltpu.SemaphoreType.DMA((2,))],
        compiler_params=_cparams(("arbitrary",), 32),
        name="moe_gather",
    )(idx3, idx3, src)


def _combine_kernel(p0_ref, p0n_ref, p1_ref, p1n_ref, ys_hbm, x_ref, gt_ref, mod_ref, nf_ref, o_ref,
                    b0, b1, sem0, sem1, *, final_norm):
    slot = _prefetched_gather([p0_ref, p1_ref], [p0n_ref, p1n_ref], ys_hbm, [b0, b1], [sem0, sem1])
    gt = gt_ref[...]
    y = gt[:, 0:1] * b0[slot] + gt[:, 1:2] * b1[slot]
    xn = x_ref[...] + mod_ref[0, 5:6, :] * y
    if final_norm:
        xn = _rms(xn) * nf_ref[...]
    o_ref[...] = xn


def _moe_combine(ys, pos0, pos1, x2, gates, mod, norm_final, l, L, final_norm):
    T, D = x2.shape
    rt = GATHER_ROWS
    per_b = L // rt
    cur, nxt = _idx_specs(T // rt, rt)
    p0 = pos0.reshape(T // rt, 1, rt)
    p1 = pos1.reshape(T // rt, 1, rt)
    return pl.pallas_call(
        functools.partial(_combine_kernel, final_norm=final_norm),
        out_shape=jax.ShapeDtypeStruct((T, D), f32),
        grid=(T // rt,),
        in_specs=[cur, nxt, cur, nxt, pl.BlockSpec(memory_space=pl.ANY),
                  pl.BlockSpec((rt, D), lambda i: (i, 0)),
                  pl.BlockSpec((rt, LANE), lambda i: (i, 0)),
                  pl.BlockSpec((None, 1, 6, D), _mod_spec(l, per_b, 1)),
                  pl.BlockSpec((1, D), lambda i: (0, 0))],
        out_specs=pl.BlockSpec((rt, D), lambda i: (i, 0)),
        scratch_shapes=[pltpu.VMEM((2, rt, D), f32), pltpu.VMEM((2, rt, D), f32),
                        pltpu.SemaphoreType.DMA((2,)), pltpu.SemaphoreType.DMA((2,))],
        compiler_params=_cparams(("arbitrary",), 40),
        name="moe_combine",
    )(p0, p0, p1, p1, ys, x2, gates, mod, norm_final)


def _moe_plan(route, T):
    tm = MOE_TM
    P = 2 * T + N_EXPERTS * tm
    e_flat = route[:, 2:4].astype(i32).reshape(-1)
    onehot = (e_flat[:, None] == jnp.arange(N_EXPERTS, dtype=i32)[None, :]).astype(i32)
    csum = jnp.cumsum(onehot, axis=0)
    counts = csum[-1]
    rank = jnp.sum(csum * onehot, axis=1) - 1
    tiles = (counts + tm - 1) // tm
    tile_end = jnp.cumsum(tiles)
    off = (tile_end - tiles) * tm
    pos = off[e_flat] + rank
    tok = jnp.arange(2 * T, dtype=i32) // 2
    src = jnp.zeros((P,), i32).at[pos].set(tok, unique_indices=True)
    n_used = tile_end[-1:]
    tidx = jnp.arange(P // tm, dtype=i32)
    te = jnp.sum((tidx[:, None] >= tile_end[None, :]).astype(i32), axis=1)
    te_last = jnp.sum((n_used - 1 >= tile_end).astype(i32))
    te = jnp.where(tidx < n_used, jnp.minimum(te, N_EXPERTS - 1), te_last).astype(i32)
    pos2 = pos.reshape(T, 2)
    return src, pos2[:, 0], pos2[:, 1], te, n_used.astype(i32)


def _permute_w_in(w_in):
    depth, d, _ = w_in.shape
    pad = jnp.zeros((depth, d, PROJ_W - OFF_SM - 48), w_in.dtype)
    parts = [w_in[:, :, 0:3072], w_in[:, :, 4376:5144], w_in[:, :, 3584:4352], w_in[:, :, 3072:3584],
             w_in[:, :, 5168:6704], w_in[:, :, 4352:4376], w_in[:, :, 5144:5168], pad]
    return jnp.concatenate(parts, axis=-1).astype(bf16)


def _small_rows(gdn_p, ssm_p):
    depth = gdn_p.shape[0]
    row = jnp.zeros((depth, LANE), f32)
    row = row.at[:, SM_GA:SM_GA + 2 * GDN_HEADS].set(gdn_p.reshape(depth, -1).astype(f32))
    row = row.at[:, SM_DT:SM_DT + 2 * SSM_HEADS].set(ssm_p.reshape(depth, -1).astype(f32))
    return row.reshape(depth, 1, LANE)


def kernel(x, c, ada_w, ada_b, norm_mix, norm_ffn, norm_final, w_in, conv_w, conv_b, gdn_A_log, gdn_dt_bias, gdn_norm, ssm_A_log, ssm_dt_bias, ssm_D, ssm_norm, na_rpb, w_out, ffn_w1, ffn_w3, ffn_w2, moe_router, moe_w1, moe_w3, moe_w2):
    bsz, L, D = x.shape
    depth = w_in.shape[0]
    T = bsz * L

    mod = _ada_mod(c, ada_w, ada_b).reshape(depth, bsz, 6, D)
    w_in_p = _permute_w_in(w_in)
    w_out_b = w_out.astype(bf16)
    alog_rows = _small_rows(gdn_A_log, ssm_A_log)
    dtb_rows = _small_rows(gdn_dt_bias, ssm_dt_bias)
    dskip_rows = jnp.repeat(ssm_D.astype(f32), SSM_HEADDIM, axis=-1).reshape(depth, 1, SSM_WIDTH)
    bias_tabs = _na_bias_tables(na_rpb)
    ffn_w1b, ffn_w3b, ffn_w2b = ffn_w1.astype(bf16), ffn_w3.astype(bf16), ffn_w2.astype(bf16)
    wr_pad = jnp.pad(moe_router.astype(f32), ((0, 0), (0, 0), (0, LANE - N_EXPERTS)))

    x2 = x.reshape(T, D)
    for l in range(depth):
        is_moe = l % 2 == 1
        proj = _inproj(x2, mod, norm_mix[l].reshape(1, D), w_in_p, l, L).reshape(bsz, L, PROJ_W)
        prep = _mix_prep(proj, alog_rows[l], dtb_rows[l])
        cb = conv_b[l].reshape(1, CONV_CH)
        ya = _gdn_mixer(proj, prep, conv_w[l], cb, gdn_norm[l].reshape(1, LANE))
        yb = _ssd_mixer(proj, prep, conv_w[l], cb, dskip_rows[l])
        yc = _na_mixer(proj, bias_tabs, l)
        x2, h2 = _outproj(ya.reshape(T, GDN_WIDTH), yb.reshape(T, SSM_WIDTH), yc.reshape(T, NA_WIDTH),
                          ssm_norm[l].reshape(1, SSM_WIDTH), w_out_b, x2, mod, norm_ffn[l].reshape(1, D),
                          l, L, f32 if is_moe else bf16)
        m = l // 2
        if not is_moe:
            x2 = _ffn_dense(h2, ffn_w1b, ffn_w3b, ffn_w2b, m, x2, mod, l, L)
        else:
            route = _router(h2, wr_pad, m)
            src, pos0, pos1, te, n_used = _moe_plan(route, T)
            hs = _gather_rows(h2, src, bf16)
            ys = _moe_ffn(hs, moe_w1, moe_w3, moe_w2, m, te, n_used)
            x2 = _moe_combine(ys, pos0, pos1, x2, route, mod, norm_final.reshape(1, D), l, L,
                              final_norm=(l == depth - 1))
    if depth % 2 == 1:
        raise NotImplementedError("the final norm is fused into the last MoE combine")
    return x2.reshape(bsz, L, D)
```

```python
import functools
import math

import jax
import jax.numpy as jnp
import numpy as np
from jax import lax
from jax.experimental import pallas as pl
from jax.experimental.pallas import tpu as pltpu

f32 = jnp.float32
bf16 = jnp.bfloat16
i32 = jnp.int32

GRID_W = 64
GDN_HEADS = 6
GDN_DK = 128
GDN_WIDTH = 768
SSM_HEADS = 12
SSM_HEADDIM = 64
SSM_WIDTH = 768
SSM_GROUPS = 2
SSM_HPG = 6
SSM_STATE = 128
NA_HEADS = 8
NA_HEADDIM = 64
NA_WIDTH = 512
NA_KH = 8
NA_KW = 16
CONV_K = 5
CONV_CH = 3584
N_EXPERTS = 8
EPS = 1e-6

OFF_GQ, OFF_GK, OFF_GV, OFF_SX, OFF_SZ, OFF_GZ = 0, 768, 1536, 2304, 3072, 3840
OFF_SB, OFF_SC, OFF_NQ, OFF_NK, OFF_NV, OFF_SM = 4608, 4864, 5120, 5632, 6144, 6656
PROJ_W = 6912
SM_BETA, SM_GA, SM_DT = 0, 12, 24

LANE = 128
MIX_CHUNK = 128
CONV_HALO = 8
MIB = 1024 * 1024


def _cparams(sem, vmem_mib):
    return pltpu.CompilerParams(dimension_semantics=sem, vmem_limit_bytes=vmem_mib * MIB)


def _sigmoid(x):
    return 1.0 / (1.0 + jnp.exp(-x))


def _softplus(x):
    return jnp.maximum(x, 0.0) + jnp.log1p(jnp.exp(-jnp.abs(x)))


def _dot(a, b):
    return jnp.dot(a.astype(bf16), b.astype(bf16), preferred_element_type=f32)


def _dot_nt(a, b):
    return lax.dot_general(a.astype(bf16), b.astype(bf16), (((1,), (1,)), ((), ())),
                           preferred_element_type=f32)


def _split(x, terms):
    out = []
    for _ in range(terms - 1):
        piece = x.astype(bf16)
        out.append(piece)
        x = x - piece.astype(f32)
    out.append(x.astype(bf16))
    return out


def _sel_left(m01, x, terms=3):
    m = m01.astype(bf16)
    return sum(jnp.dot(m, t, preferred_element_type=f32) for t in _split(x, terms))


def _sel_right(x, m01, terms=2):
    m = m01.astype(bf16)
    return sum(jnp.dot(t, m, preferred_element_type=f32) for t in _split(x, terms))


def _rms(x):
    return x * lax.rsqrt(jnp.mean(x * x, axis=-1, keepdims=True) + EPS)


def _conv_silu(xp_ref, cw_ref, cb_ref, r0, rows, blk=0):
    lanes = slice(blk * LANE, (blk + 1) * LANE)
    acc = jnp.broadcast_to(cb_ref[:, lanes], (rows, LANE))
    for j in range(CONV_K):
        shift = CONV_HALO - CONV_K // 2 + j
        acc = acc + cw_ref[j:j + 1, lanes] * xp_ref[blk, pl.ds(r0 + shift, rows), :]
    return acc * _sigmoid(acc)


def _fill_padded(src_ref, dst_ref, L):
    for blk in range(dst_ref.shape[0]):
        dst_ref[blk, 0:CONV_HALO, :] = jnp.zeros((CONV_HALO, LANE), f32)
        dst_ref[blk, L + CONV_HALO:L + 2 * CONV_HALO, :] = jnp.zeros((CONV_HALO, LANE), f32)
        dst_ref[blk, CONV_HALO:L + CONV_HALO, :] = src_ref[0, :, blk * LANE:(blk + 1) * LANE]


def _pick_col(blk, idx):
    lane = lax.broadcasted_iota(i32, blk.shape, 1)
    return jnp.sum(jnp.where(lane == idx, blk, 0.0), axis=-1, keepdims=True)


def _ada_kernel(c_ref, w_ref, b_ref, o_ref):
    c = c_ref[...]
    cond = c * _sigmoid(c)
    o_ref[0] = _dot(cond, w_ref[0]) + b_ref[0]


def _ada_mod(c, ada_w, ada_b):
    depth, d, n = ada_w.shape
    bsz = c.shape[0]
    tn = 1536
    return pl.pallas_call(
        _ada_kernel,
        out_shape=jax.ShapeDtypeStruct((depth, bsz, n), f32),
        grid=(depth, n // tn),
        in_specs=[pl.BlockSpec((bsz, d), lambda l, j: (0, 0)),
                  pl.BlockSpec((1, d, tn), lambda l, j: (l, 0, j)),
                  pl.BlockSpec((1, 1, tn), lambda l, j: (l, 0, j))],
        out_specs=pl.BlockSpec((1, bsz, tn), lambda l, j: (l, 0, j)),
        compiler_params=_cparams(("arbitrary", "arbitrary"), 40),
        name="ada_mod",
    )(c, ada_w, ada_b.reshape(depth, 1, n))


NORM_ROWS = 128


def _norm_mod_rows(x_ref, nw, sc, sh, h_ref):
    def body(i, carry):
        r = pl.multiple_of(i * NORM_ROWS, NORM_ROWS)
        y = _rms(x_ref[pl.ds(r, NORM_ROWS), :]) * nw
        h_ref[pl.ds(r, NORM_ROWS), :] = (y * (1.0 + sc) + sh).astype(h_ref.dtype)
        return carry
    lax.fori_loop(0, x_ref.shape[0] // NORM_ROWS, body, 0)


def _inproj_kernel(x_ref, mod_ref, nw_ref, w_ref, o_ref, h_ref):
    @pl.when(pl.program_id(1) == 0)
    def _():
        _norm_mod_rows(x_ref, nw_ref[...], mod_ref[0, 1:2, :], mod_ref[0, 0:1, :], h_ref)
    o_ref[...] = jnp.dot(h_ref[...], w_ref[...], preferred_element_type=f32)


def _mod_spec(l, per_b, nargs):
    if nargs == 1:
        return lambda i: (l, i // per_b, 0, 0)
    return lambda i, j: (l, i // per_b, 0, 0)


def _inproj(x2, mod, nw, w_all, l, L):
    T, D = x2.shape
    NP = w_all.shape[2]
    tm, tn = min(1024, L), 768
    per_b = L // tm
    return pl.pallas_call(
        _inproj_kernel,
        out_shape=jax.ShapeDtypeStruct((T, NP), f32),
        grid=(T // tm, NP // tn),
        in_specs=[pl.BlockSpec((tm, D), lambda i, j: (i, 0)),
                  pl.BlockSpec((None, 1, 6, D), _mod_spec(l, per_b, 2)),
                  pl.BlockSpec((1, D), lambda i, j: (0, 0)),
                  pl.BlockSpec((None, D, tn), lambda i, j: (l, 0, j))],
        out_specs=pl.BlockSpec((tm, tn), lambda i, j: (i, j)),
        scratch_shapes=[pltpu.VMEM((tm, D), bf16)],
        compiler_params=_cparams(("arbitrary", "arbitrary"), 44),
        name="inproj",
    )(x2, mod, nw, w_all)


def _prep_kernel(sm_ref, alog_ref, dtb_ref, act_ref, cum_ref, cumt_ref, tot_ref):
    L = sm_ref.shape[1]
    C = MIX_CHUNK
    ri = lax.broadcasted_iota(i32, (C, C), 0)
    ci = lax.broadcasted_iota(i32, (C, C), 1)
    trilf = jnp.where(ri >= ci, 1.0, 0.0)
    lane = lax.broadcasted_iota(i32, (C, LANE), 1)
    gdn_rev = jnp.logical_and(lane >= SM_GA + GDN_HEADS, lane < SM_DT)
    backward = jnp.logical_or(gdn_rev, lane >= SM_DT + SSM_HEADS)
    neg_a = -jnp.exp(alog_ref[...])
    dtb = dtb_ref[...]

    def body(c, carry):
        r0 = pl.multiple_of(c * C, C)
        sm = sm_ref[0, pl.ds(r0, C), :]
        sp = _softplus(sm + dtb)
        act_ref[0, pl.ds(r0, C), :] = jnp.where(lane < SM_GA, _sigmoid(sm), sp)
        a = neg_a * sp
        cum = _sel_left(trilf, a)
        tot = cum[C - 1:C, :]
        cd = jnp.where(backward, tot - cum + a, cum)
        cum_ref[0, pl.ds(r0, C), :] = cd
        cumt_ref[0, c] = cd.T
        tot_ref[0, c] = jnp.broadcast_to(tot, (8, LANE))
        return carry

    lax.fori_loop(0, L // C, body, 0)


def _mix_prep(proj, alog_row, dtb_row):
    bsz, L, _ = proj.shape
    C = MIX_CHUNK
    nc = L // C
    row = pl.BlockSpec((1, LANE), lambda b: (0, 0))
    tok = pl.BlockSpec((1, L, LANE), lambda b: (b, 0, 0))
    return pl.pallas_call(
        _prep_kernel,
        out_shape=(jax.ShapeDtypeStruct((bsz, L, LANE), f32), jax.ShapeDtypeStruct((bsz, L, LANE), f32),
                   jax.ShapeDtypeStruct((bsz, nc, LANE, C), f32), jax.ShapeDtypeStruct((bsz, nc, 8, LANE), f32)),
        grid=(bsz,),
        in_specs=[pl.BlockSpec((1, L, LANE), lambda b: (b, 0, OFF_SM // LANE)), row, row],
        out_specs=(tok, tok, pl.BlockSpec((1, nc, LANE, C), lambda b: (b, 0, 0, 0)),
                   pl.BlockSpec((1, nc, 8, LANE), lambda b: (b, 0, 0, 0))),
        compiler_params=_cparams(("arbitrary",), 32),
        name="mix_prep",
    )(proj, alog_row, dtb_row)


GDN_GROUP = 8
INV_BASE = 16


def _unit_tri_inverses(a_mats):
    C = a_mats[0].shape[0]
    ri = lax.broadcasted_iota(i32, (C, C), 0)
    ci = lax.broadcasted_iota(i32, (C, C), 1)
    eye = jnp.where(ri == ci, 1.0, 0.0)

    def same(b):
        s = int(math.log2(b))
        return lax.shift_right_logical(ri, s) == lax.shift_right_logical(ci, s)

    ps = [jnp.where(same(INV_BASE), a, 0.0) for a in a_mats]
    ts = [eye - p for p in ps]
    for _ in range(int(math.log2(INV_BASE)) - 1):
        ps = [_dot(p, p) for p in ps]
        ts = [t + _dot(t, p) for t, p in zip(ts, ps)]
    b = INV_BASE
    while b < C:
        off = jnp.logical_and(same(2 * b), jnp.logical_not(same(b)))
        us = [_dot(jnp.where(off, a, 0.0), t) for a, t in zip(a_mats, ts)]
        ts = [t - _dot(t, u) for t, u in zip(ts, us)]
        b *= 2
    return ts


def _gdn_kernel(q_ref, k_ref, v_ref, z_ref, act_ref, cum_ref, cumt_ref, tot_ref,
                cwq, cwk, cwv, cbq, cbk, cbv, nw_ref, o_ref,
                qp, kp, vp, kmat_s, nmat_s, qeff_s, gl_s, oacc):
    h = pl.program_id(1)
    L = q_ref.shape[1]
    C = MIX_CHUNK
    nc = L // C
    group = math.gcd(GDN_GROUP, nc)
    _fill_padded(q_ref, qp, L)
    _fill_padded(k_ref, kp, L)
    _fill_padded(v_ref, vp, L)
    oacc[...] = jnp.zeros(oacc.shape, f32)

    def phase_a(gi, carry):
        ri = lax.broadcasted_iota(i32, (C, C), 0)
        ci = lax.broadcasted_iota(i32, (C, C), 1)
        chains = []
        for j in range(group):
            c = gi * group + j
            r0 = pl.multiple_of(c * C, C)
            q = _conv_silu(qp, cwq, cbq, r0, C)
            k = _conv_silu(kp, cwk, cbk, r0, C)
            v = _conv_silu(vp, cwv, cbv, r0, C)
            qn = q * lax.rsqrt(jnp.sum(q * q, axis=-1, keepdims=True) + EPS) * (GDN_DK ** -0.5)
            kn = k * lax.rsqrt(jnp.sum(k * k, axis=-1, keepdims=True) + EPS)
            kk = _dot_nt(kn, kn)
            qk = _dot_nt(qn, kn)
            act = act_ref[0, pl.ds(r0, C), :]
            cum = cum_ref[0, pl.ds(r0, C), :]
            for d in (0, 1):
                gidx = SM_GA + d * GDN_HEADS + h
                beta = _pick_col(act, SM_BETA + d * GDN_HEADS + h)
                gcol = _pick_col(cum, gidx)
                grow = cumt_ref[0, c, pl.ds(gidx, 1), :]
                gl = _pick_col(tot_ref[0, c, 0:1, :], gidx)
                mask = (ri >= ci) if d == 0 else (ri <= ci)
                strict = (ri > ci) if d == 0 else (ri < ci)
                dec = jnp.exp(jnp.where(mask, gcol - grow, -jnp.inf))
                a_mat = jnp.where(strict, beta * kk * dec, 0.0)
                chains.append((c, r0, d, a_mat, dec, beta, gcol, gl, qn, kn, v, qk))
        t_invs = _unit_tri_inverses([ch[3] for ch in chains])
        for (c, r0, d, _, dec, beta, gcol, gl, qn, kn, v, qk), t_inv in zip(chains, t_invs):
            eg = jnp.exp(gcol)
            sol = _dot(t_inv, jnp.concatenate([v * beta, kn * (beta * eg)], axis=1))
            kd = kn * jnp.exp(gl - gcol)
            big = _dot(jnp.concatenate([kd.T, qk * dec], axis=0), sol)
            nmat_s[d, c] = big[0:GDN_DK, 0:LANE]
            kmat_s[d, c] = big[0:GDN_DK, LANE:].astype(bf16)
            qeff_s[d, pl.ds(r0, C), :] = (qn * eg - big[GDN_DK:, LANE:]).astype(bf16)
            oacc[pl.ds(r0, C), :] += big[GDN_DK:, 0:LANE]
            gl_s[d, c] = jnp.broadcast_to(jnp.exp(gl), (8, LANE))
        return carry

    lax.fori_loop(0, nc // group, phase_a, 0)

    def phase_b(i, carry):
        out = []
        for d in (0, 1):
            s = carry[d]
            c = i if d == 0 else nc - 1 - i
            r0 = pl.multiple_of(c * C, C)
            sb = s.astype(bf16)
            oacc[pl.ds(r0, C), :] += jnp.dot(qeff_s[d, pl.ds(r0, C), :], sb, preferred_element_type=f32)
            out.append(s * gl_s[d, c][0:1, :] - jnp.dot(kmat_s[d, c], sb, preferred_element_type=f32)
                       + nmat_s[d, c])
        return tuple(out)

    s0 = jnp.zeros((GDN_DK, LANE), f32)
    lax.fori_loop(0, nc, phase_b, (s0, s0))

    def fin(c, carry):
        r0 = pl.multiple_of(c * C, C)
        y = _rms(oacc[pl.ds(r0, C), :]) * nw_ref[...]
        z = z_ref[0, pl.ds(r0, C), :]
        o_ref[0, pl.ds(r0, C), :] = (y * (z * _sigmoid(z))).astype(o_ref.dtype)
        return carry

    lax.fori_loop(0, nc, fin, 0)


def _prep_specs(L, nc, C):
    tok = pl.BlockSpec((1, L, LANE), lambda b, h: (b, 0, 0))
    return [tok, tok, pl.BlockSpec((1, nc, LANE, C), lambda b, h: (b, 0, 0, 0)),
            pl.BlockSpec((1, nc, 8, LANE), lambda b, h: (b, 0, 0, 0))]


def _gdn_mixer(proj, prep, conv_w, conv_b, gdn_norm):
    bsz, L, _ = proj.shape
    C = MIX_CHUNK
    nc = L // C
    blk = lambda off: pl.BlockSpec((1, L, LANE), lambda b, h, o=off // LANE: (b, 0, o + h))
    cwb = lambda off: pl.BlockSpec((CONV_K, LANE), lambda b, h, o=off // LANE: (0, o + h))
    cbb = lambda off: pl.BlockSpec((1, LANE), lambda b, h, o=off // LANE: (0, o + h))
    return pl.pallas_call(
        _gdn_kernel,
        out_shape=jax.ShapeDtypeStruct((bsz, L, GDN_WIDTH), bf16),
        grid=(bsz, GDN_HEADS),
        in_specs=[blk(OFF_GQ), blk(OFF_GK), blk(OFF_GV), blk(OFF_GZ)] + _prep_specs(L, nc, C) + [
            cwb(0), cwb(768), cwb(1536), cbb(0), cbb(768), cbb(1536),
            pl.BlockSpec((1, LANE), lambda b, h: (0, 0))],
        out_specs=pl.BlockSpec((1, L, LANE), lambda b, h: (b, 0, h)),
        scratch_shapes=[pltpu.VMEM((1, L + 2 * CONV_HALO, LANE), f32)] * 3 + [
            pltpu.VMEM((2, nc, GDN_DK, LANE), bf16),
            pltpu.VMEM((2, nc, GDN_DK, LANE), f32),
            pltpu.VMEM((2, L, LANE), bf16),
            pltpu.VMEM((2, nc, 8, LANE), f32),
            pltpu.VMEM((L, LANE), f32)],
        compiler_params=_cparams(("arbitrary", "arbitrary"), 40),
        name="gdn_mixer",
    )(proj, proj, proj, proj, *prep, conv_w, conv_w, conv_w, conv_b, conv_b, conv_b, gdn_norm)


SSD_W = SSM_HPG * SSM_HEADDIM


def _ssd_kernel(x_ref, z_ref, b_ref, c_ref, act_ref, cum_ref, cumt_ref, tot_ref,
                cwx, cwb, cwc, cbx, cbb, cbc, dsk_ref, o_ref, xp, bp, cp, xc, bc, cc, yacc):
    g = pl.program_id(1)
    L = x_ref.shape[1]
    C = MIX_CHUNK
    nc = L // C
    _fill_padded(x_ref, xp, L)
    _fill_padded(b_ref, bp, L)
    _fill_padded(c_ref, cp, L)
    yacc[...] = jnp.zeros(yacc.shape, f32)

    def conv_all(c, carry):
        r0 = pl.multiple_of(c * C, C)
        for blk in range(SSD_W // LANE):
            xc[pl.ds(r0, C), blk * LANE:(blk + 1) * LANE] = _conv_silu(xp, cwx, cbx, r0, C, blk)
        bc[pl.ds(r0, C), :] = _conv_silu(bp, cwb, cbb, r0, C)
        cc[pl.ds(r0, C), :] = _conv_silu(cp, cwc, cbc, r0, C)
        return carry

    lax.fori_loop(0, nc, conv_all, 0)

    def step(i, carry):
        ri = lax.broadcasted_iota(i32, (C, C), 0)
        ci = lax.broadcasted_iota(i32, (C, C), 1)
        er = lax.broadcasted_iota(i32, (LANE, SSD_W), 0)
        el = lax.shift_right_logical(lax.broadcasted_iota(i32, (LANE, SSD_W), 1),
                                     int(math.log2(SSM_HEADDIM)))
        half = lax.broadcasted_iota(i32, (C, LANE), 1) < SSM_HEADDIM
        out = []
        for d in (0, 1):
            ht = carry[d]
            c = i if d == 0 else nc - 1 - i
            r0 = pl.multiple_of(c * C, C)
            base = SM_DT + d * SSM_HEADS + g * SSM_HPG
            expand = jnp.where(er == base + el, 1.0, 0.0)
            cum = cum_ref[0, pl.ds(r0, C), :]
            g6 = _sel_right(cum, expand)
            dt6 = _sel_right(act_ref[0, pl.ds(r0, C), :], expand)
            tot6 = _sel_right(tot_ref[0, c], expand)[0:1, :]
            xdt = xc[pl.ds(r0, C), :] * dt6
            bm = bc[pl.ds(r0, C), :]
            cm = cc[pl.ds(r0, C), :]
            cb = _dot_nt(cm, bm)
            mask = (ri >= ci) if d == 0 else (ri <= ci)
            pieces = []
            for pr in range(SSM_HPG // 2):
                xpair = xdt[:, pr * LANE:(pr + 1) * LANE].astype(bf16)
                ys = []
                for e in (0, 1):
                    gidx = base + 2 * pr + e
                    diff = _pick_col(cum, gidx) - cumt_ref[0, c, pl.ds(gidx, 1), :]
                    lm = jnp.exp(jnp.where(mask, diff, -jnp.inf))
                    ys.append(jnp.dot((cb * lm).astype(bf16), xpair, preferred_element_type=f32))
                pieces.append(jnp.where(half, ys[0], ys[1]))
            y_diag = jnp.concatenate(pieces, axis=1)
            y_off = _dot(cm, ht) * jnp.exp(g6)
            yacc[pl.ds(r0, C), :] += y_diag + y_off
            st = _dot(bm.T, xdt * jnp.exp(tot6 - g6))
            out.append(ht * jnp.exp(tot6) + st)
        return tuple(out)

    h0 = jnp.zeros((SSM_STATE, SSD_W), f32)
    lax.fori_loop(0, nc, step, (h0, h0), unroll=2)

    def fin(c, carry):
        r0 = pl.multiple_of(c * C, C)
        y = yacc[pl.ds(r0, C), :] + xc[pl.ds(r0, C), :] * dsk_ref[...]
        z = z_ref[0, pl.ds(r0, C), :]
        o_ref[0, pl.ds(r0, C), :] = y * (z * _sigmoid(z))
        return carry

    lax.fori_loop(0, nc, fin, 0)


def _ssd_mixer(proj, prep, conv_w, conv_b, dskip_row):
    bsz, L, _ = proj.shape
    C = MIX_CHUNK
    nc = L // C
    wide = lambda off: pl.BlockSpec((1, L, SSD_W), lambda b, g, o=off // SSD_W: (b, 0, o + g))
    nar = lambda off: pl.BlockSpec((1, L, LANE), lambda b, g, o=off // LANE: (b, 0, o + g))
    return pl.pallas_call(
        _ssd_kernel,
        out_shape=jax.ShapeDtypeStruct((bsz, L, SSM_WIDTH), f32),
        grid=(bsz, SSM_GROUPS),
        in_specs=[wide(OFF_SX), wide(OFF_SZ), nar(OFF_SB), nar(OFF_SC)] + _prep_specs(L, nc, C) + [
            pl.BlockSpec((CONV_K, SSD_W), lambda b, g: (0, 2304 // SSD_W + g)),
            pl.BlockSpec((CONV_K, LANE), lambda b, g: (0, 3072 // LANE + g)),
            pl.BlockSpec((CONV_K, LANE), lambda b, g: (0, 3328 // LANE + g)),
            pl.BlockSpec((1, SSD_W), lambda b, g: (0, 2304 // SSD_W + g)),
            pl.BlockSpec((1, LANE), lambda b, g: (0, 3072 // LANE + g)),
            pl.BlockSpec((1, LANE), lambda b, g: (0, 3328 // LANE + g)),
            pl.BlockSpec((1, SSD_W), lambda b, g: (0, g))],
        out_specs=pl.BlockSpec((1, L, SSD_W), lambda b, g: (b, 0, g)),
        scratch_shapes=[pltpu.VMEM((SSD_W // LANE, L + 2 * CONV_HALO, LANE), f32),
                        pltpu.VMEM((1, L + 2 * CONV_HALO, LANE), f32),
                        pltpu.VMEM((1, L + 2 * CONV_HALO, LANE), f32),
                        pltpu.VMEM((L, SSD_W), f32),
                        pltpu.VMEM((L, LANE), f32),
                        pltpu.VMEM((L, LANE), f32),
                        pltpu.VMEM((L, SSD_W), f32)],
        compiler_params=_cparams(("arbitrary", "arbitrary"), 48),
        name="ssd_mixer",
    )(proj, proj, proj, proj, *prep, conv_w, conv_w, conv_w, conv_b, conv_b, conv_b, dskip_row)


NA_WIN = NA_KH * GRID_W
NA_UNROLL = 4


def _na_kernel(q_ref, k_ref, v_ref, bias_ref, o_ref, kb, vb, s_scr, p_scr, den_scr):
    L = q_ref.shape[1]
    rows = L // GRID_W
    kb[...] = k_ref[0].astype(bf16)
    vb[...] = v_ref[0].astype(bf16)

    def body(g, carry):
        low = lax.broadcasted_iota(i32, (GRID_W, LANE), 1) < NA_HEADDIM
        wins = []
        for j in range(NA_UNROLL):
            r = g * NA_UNROLL + j
            rs = jnp.clip(r - NA_KH // 2, 0, rows - NA_KH)
            off = r - rs
            q0 = pl.multiple_of(r * GRID_W, GRID_W)
            w0 = pl.multiple_of(rs * GRID_W, GRID_W)
            q = q_ref[0, pl.ds(q0, GRID_W), :] * (NA_HEADDIM ** -0.5)
            kw = kb[pl.ds(w0, NA_WIN), :]
            for e in (0, 1):
                qm = jnp.where(low if e == 0 else jnp.logical_not(low), q, 0.0)
                s_scr[j, e] = _dot_nt(qm, kw) + bias_ref[e, off]
            wins.append((q0, w0))
        for j in range(NA_UNROLL):
            for e in (0, 1):
                s = s_scr[j, e]
                p = jnp.exp(s - jnp.max(s, axis=-1, keepdims=True))
                den_scr[j, e] = jnp.broadcast_to(jnp.sum(p, axis=-1, keepdims=True), (GRID_W, LANE))
                p_scr[j, e] = p.astype(bf16)
        for j, (q0, w0) in enumerate(wins):
            vw = vb[pl.ds(w0, NA_WIN), :]
            outs = [jnp.dot(p_scr[j, e], vw, preferred_element_type=f32) / den_scr[j, e] for e in (0, 1)]
            o_ref[0, pl.ds(q0, GRID_W), :] = jnp.where(low, outs[0], outs[1]).astype(o_ref.dtype)
        return carry

    lax.fori_loop(0, rows // NA_UNROLL, body, 0)


def _na_bias_tables(rpb):
    depth = rpb.shape[0]
    o = np.arange(NA_KH)[:, None]
    i = np.arange(NA_KH)[None, :]
    dr_idx = (i - o) + NA_KH - 1
    cols = np.arange(GRID_W)
    col_start = np.clip(cols - NA_KW // 2, 0, GRID_W - NA_KW)
    in_win = (cols[None, :] >= col_start[:, None]) & (cols[None, :] < col_start[:, None] + NA_KW)
    dc_idx = np.clip(cols[None, :] - cols[:, None], -(NA_KW - 1), NA_KW - 1) + NA_KW - 1
    oh_r = (dr_idx[..., None] == np.arange(2 * NA_KH - 1)).astype(np.float32)
    oh_c = (dc_idx[..., None] == np.arange(2 * NA_KW - 1)).astype(np.float32)
    hi = lax.Precision.HIGHEST
    t = jnp.einsum('oir,lhrc->lhoic', oh_r, rpb.astype(f32), precision=hi)
    bias = jnp.einsum('lhoic,qkc->lhoqik', t, oh_c, precision=hi)
    bias = jnp.where(in_win[None, None, None, :, None, :], bias, -jnp.inf)
    return bias.reshape(depth, NA_HEADS, NA_KH, GRID_W, NA_WIN)


def _na_mixer(proj, bias_tabs, l):
    bsz, L, _ = proj.shape
    assert L // GRID_W >= NA_KH and (L // GRID_W) % NA_UNROLL == 0
    blk = lambda off: pl.BlockSpec((1, L, LANE), lambda b, p, o=off // LANE: (b, 0, o + p))
    return pl.pallas_call(
        _na_kernel,
        out_shape=jax.ShapeDtypeStruct((bsz, L, NA_WIDTH), bf16),
        grid=(bsz, NA_HEADS // 2),
        in_specs=[blk(OFF_NQ), blk(OFF_NK), blk(OFF_NV),
                  pl.BlockSpec((None, 2, NA_KH, GRID_W, NA_WIN), lambda b, p: (l, p, 0, 0, 0))],
        out_specs=pl.BlockSpec((1, L, LANE), lambda b, p: (b, 0, p)),
        scratch_shapes=[pltpu.VMEM((L, LANE), bf16), pltpu.VMEM((L, LANE), bf16),
                        pltpu.VMEM((NA_UNROLL, 2, GRID_W, NA_WIN), f32),
                        pltpu.VMEM((NA_UNROLL, 2, GRID_W, NA_WIN), bf16),
                        pltpu.VMEM((NA_UNROLL, 2, GRID_W, LANE), f32)],
        compiler_params=_cparams(("arbitrary", "arbitrary"), 32),
        name="na_mixer",
    )(proj, proj, proj, bias_tabs)


def _outproj_kernel(ya_ref, yb_ref, yc_ref, sn_ref, wa_ref, wb_ref, wc_ref, x_ref, mod_ref, nw_ref,
                    xo_ref, h2_ref, ybn):
    _norm_mod_rows(yb_ref, sn_ref[...], 0.0, 0.0, ybn)
    y = (jnp.dot(ya_ref[...], wa_ref[...], preferred_element_type=f32)
         + jnp.dot(ybn[...], wb_ref[...], preferred_element_type=f32)
         + jnp.dot(yc_ref[...], wc_ref[...], preferred_element_type=f32))
    xo_ref[...] = x_ref[...] + mod_ref[0, 2:3, :] * y
    _norm_mod_rows(xo_ref, nw_ref[...], mod_ref[0, 4:5, :], mod_ref[0, 3:4, :], h2_ref)


def _outproj(ya, yb, yc, ssm_norm, w_out_all, x2, mod, nw, l, L, h2_dtype):
    T, D = x2.shape
    tm = 512
    per_b = L // tm
    const = lambda shape: pl.BlockSpec(shape, lambda i: (0, 0), pipeline_mode=pl.Buffered(1))
    wspec = lambda rows_, blk_idx: pl.BlockSpec((None, rows_, D), lambda i: (l, blk_idx, 0),
                                                pipeline_mode=pl.Buffered(1))
    rows = lambda w: pl.BlockSpec((tm, w), lambda i: (i, 0))
    nb_off = (GDN_WIDTH + SSM_WIDTH) // NA_WIDTH
    return pl.pallas_call(
        _outproj_kernel,
        out_shape=(jax.ShapeDtypeStruct((T, D), f32), jax.ShapeDtypeStruct((T, D), h2_dtype)),
        grid=(T // tm,),
        in_specs=[rows(GDN_WIDTH), rows(SSM_WIDTH), rows(NA_WIDTH), const((1, SSM_WIDTH)),
                  wspec(GDN_WIDTH, 0), wspec(SSM_WIDTH, 1), wspec(NA_WIDTH, nb_off),
                  rows(D), pl.BlockSpec((None, 1, 6, D), _mod_spec(l, per_b, 1)), const((1, D))],
        out_specs=(rows(D), rows(D)),
        scratch_shapes=[pltpu.VMEM((tm, SSM_WIDTH), bf16)],
        compiler_params=_cparams(("arbitrary",), 48),
        name="outproj",
    )(ya, yb, yc, ssm_norm, w_out_all, w_out_all, w_out_all, x2, mod, nw)


def _swiglu_acc(h_ref, w1_ref, w3_ref, w2_ref, acc_ref):
    h = h_ref[...]
    a = jnp.dot(h, w1_ref[...].astype(bf16), preferred_element_type=f32)
    b = jnp.dot(h, w3_ref[...].astype(bf16), preferred_element_type=f32)
    u = (a * _sigmoid(a) * b).astype(bf16)
    acc_ref[...] += jnp.dot(u, w2_ref[...].astype(bf16), preferred_element_type=f32)


def _ffn_kernel(h_ref, w1_ref, w3_ref, w2_ref, x_ref, mod_ref, o_ref):
    f = pl.program_id(1)

    @pl.when(f == 0)
    def _():
        o_ref[...] = jnp.zeros(o_ref.shape, f32)

    _swiglu_acc(h_ref, w1_ref, w3_ref, w2_ref, o_ref)

    @pl.when(f == pl.num_programs(1) - 1)
    def _():
        o_ref[...] = x_ref[...] + mod_ref[0, 5:6, :] * o_ref[...]


FFN_TM = 1024


def _ffn_dense(h2, w1_all, w3_all, w2_all, m, x2, mod, l, L):
    T, D = x2.shape
    F = w1_all.shape[2]
    tm, tf = min(FFN_TM, L), 512
    per_b = L // tm
    return pl.pallas_call(
        _ffn_kernel,
        out_shape=jax.ShapeDtypeStruct((T, D), f32),
        grid=(T // tm, F // tf),
        in_specs=[pl.BlockSpec((tm, D), lambda i, f: (i, 0)),
                  pl.BlockSpec((None, D, tf), lambda i, f: (m, 0, f)),
                  pl.BlockSpec((None, D, tf), lambda i, f: (m, 0, f)),
                  pl.BlockSpec((None, tf, D), lambda i, f: (m, f, 0)),
                  pl.BlockSpec((tm, D), lambda i, f: (i, 0)),
                  pl.BlockSpec((None, 1, 6, D), _mod_spec(l, per_b, 2))],
        out_specs=pl.BlockSpec((tm, D), lambda i, f: (i, 0)),
        compiler_params=_cparams(("arbitrary", "arbitrary"), 60),
        name="ffn_dense",
    )(h2, w1_all, w3_all, w2_all, x2, mod)


MOE_TM = FFN_TM


def _moe_ffn_kernel(te_ref, nu_ref, h_ref, w1_ref, w3_ref, w2_ref, o_ref):
    i = pl.program_id(0)
    f = pl.program_id(1)

    @pl.when(f == 0)
    def _():
        o_ref[...] = jnp.zeros(o_ref.shape, f32)

    @pl.when(i < nu_ref[0])
    def _():
        _swiglu_acc(h_ref, w1_ref, w3_ref, w2_ref, o_ref)


def _moe_ffn(hs, w1_all, w3_all, w2_all, m, tile_expert, n_used):
    P, D = hs.shape
    F = w1_all.shape[3]
    tm, tf = MOE_TM, 512
    nf = F // tf

    def fe(i, f, nu):
        return jnp.where(i < nu[0], f, nf - 1)

    grid_spec = pltpu.PrefetchScalarGridSpec(
        num_scalar_prefetch=2,
        grid=(P // tm, nf),
        in_specs=[pl.BlockSpec((tm, D), lambda i, f, te, nu: (jnp.minimum(i, nu[0] - 1), 0)),
                  pl.BlockSpec((None, None, D, tf), lambda i, f, te, nu: (m, te[i], 0, fe(i, f, nu))),
                  pl.BlockSpec((None, None, D, tf), lambda i, f, te, nu: (m, te[i], 0, fe(i, f, nu))),
                  pl.BlockSpec((None, None, tf, D), lambda i, f, te, nu: (m, te[i], fe(i, f, nu), 0))],
        out_specs=pl.BlockSpec((tm, D), lambda i, f, te, nu: (i, 0)))
    return pl.pallas_call(
        _moe_ffn_kernel,
        out_shape=jax.ShapeDtypeStruct((P, D), f32),
        grid_spec=grid_spec,
        compiler_params=_cparams(("arbitrary", "arbitrary"), 60),
        name="moe_ffn",
    )(tile_expert, n_used, hs, w1_all, w3_all, w2_all)


def _router_kernel(h_ref, wr_ref, o_ref):
    hh, hm = _split(h_ref[...], 2)
    wh, wm = _split(wr_ref[...], 2)
    logits = (jnp.dot(hh, wh, preferred_element_type=f32) + jnp.dot(hh, wm, preferred_element_type=f32)
              + jnp.dot(hm, wh, preferred_element_type=f32))
    lane = lax.broadcasted_iota(i32, logits.shape, 1)
    logits = jnp.where(lane < N_EXPERTS, logits, -jnp.inf)
    m1 = jnp.max(logits, axis=-1, keepdims=True)
    i1 = jnp.min(jnp.where(logits == m1, lane, LANE), axis=-1, keepdims=True)
    rest = jnp.where(lane == i1, -jnp.inf, logits)
    m2 = jnp.max(rest, axis=-1, keepdims=True)
    i2 = jnp.min(jnp.where(rest == m2, lane, LANE), axis=-1, keepdims=True)
    e2 = jnp.exp(m2 - m1)
    g1 = 1.0 / (1.0 + e2)
    g2 = e2 / (1.0 + e2)
    o_ref[...] = jnp.where(lane == 0, g1, jnp.where(lane == 1, g2, jnp.where(
        lane == 2, i1.astype(f32), jnp.where(lane == 3, i2.astype(f32), 0.0))))


def _router(h2, wr_pad_all, m):
    T, D = h2.shape
    tm = 512
    return pl.pallas_call(
        _router_kernel,
        out_shape=jax.ShapeDtypeStruct((T, LANE), f32),
        grid=(T // tm,),
        in_specs=[pl.BlockSpec((tm, D), lambda i: (i, 0)),
                  pl.BlockSpec((None, D, LANE), lambda i: (m, 0, 0))],
        out_specs=pl.BlockSpec((tm, LANE), lambda i: (i, 0)),
        compiler_params=_cparams(("arbitrary",), 32),
        name="moe_router",
    )(h2, wr_pad_all)


GATHER_ROWS = 256


def _row_copy(src_hbm, buf, sem, t, r):
    return pltpu.make_async_copy(src_hbm.at[pl.ds(t, 1), :], buf.at[pl.ds(r, 1), :], sem)


ROW_UNROLL = 8


def _issue_rows(idx_ref, src_hbm, buf, sem):
    def issue(p, carry):
        for prio in (0, 1):
            r = 2 * p + prio
            _row_copy(src_hbm, buf, sem, idx_ref[0, 0, r], r).start(priority=prio)
        return carry

    lax.fori_loop(0, buf.shape[0] // 2, issue, 0, unroll=ROW_UNROLL // 2)


def _wait_rows(src_hbm, buf, sem):
    def wait(r, carry):
        _row_copy(src_hbm, buf, sem, 0, r).wait()
        return carry

    lax.fori_loop(0, buf.shape[0], wait, 0, unroll=ROW_UNROLL)


def _prefetched_gather(step_idx_refs, next_idx_refs, src_hbm, bufs, sems):
    i = pl.program_id(0)
    slot = lax.rem(i, 2)
    for idx_ref, buf, sem in zip(step_idx_refs, bufs, sems):
        @pl.when(i == 0)
        def _(idx_ref=idx_ref, buf=buf, sem=sem):
            _issue_rows(idx_ref, src_hbm, buf.at[0], sem.at[0])
    for idx_ref, buf, sem in zip(next_idx_refs, bufs, sems):
        @pl.when(i + 1 < pl.num_programs(0))
        def _(idx_ref=idx_ref, buf=buf, sem=sem):
            _issue_rows(idx_ref, src_hbm, buf.at[1 - slot], sem.at[1 - slot])
    for buf, sem in zip(bufs, sems):
        _wait_rows(src_hbm, buf.at[slot], sem.at[slot])
    return slot


def _gather_kernel(idx_ref, nxt_ref, src_hbm, o_ref, buf, sem):
    slot = _prefetched_gather([idx_ref], [nxt_ref], src_hbm, [buf], [sem])
    o_ref[...] = buf[slot].astype(o_ref.dtype)


def _idx_specs(n_tiles, rt):
    cur = pl.BlockSpec((1, 1, rt), lambda i: (i, 0, 0), memory_space=pltpu.SMEM)
    nxt = pl.BlockSpec((1, 1, rt), lambda i: (jnp.minimum(i + 1, n_tiles - 1), 0, 0),
                       memory_space=pltpu.SMEM)
    return cur, nxt


def _gather_rows(src, idx, out_dtype):
    P = idx.shape[0]
    D = src.shape[1]
    rt = GATHER_ROWS
    cur, nxt = _idx_specs(P // rt, rt)
    idx3 = idx.reshape(P // rt, 1, rt)
    return pl.pallas_call(
        _gather_kernel,
        out_shape=jax.ShapeDtypeStruct((P, D), out_dtype),
        grid=(P // rt,),
        in_specs=[cur, nxt, pl.BlockSpec(memory_space=pl.ANY)],
        out_specs=pl.BlockSpec((rt, D), lambda i: (i, 0)),
        scratch_shapes=[pltpu.VMEM((2, rt, D), f32), pltpu.SemaphoreType.DMA((2,))],
        compiler_params=_cparams(("arbitrary",), 32),
        name="moe_gather",
    )(idx3, idx3, src)


def _combine_kernel(p0_ref, p0n_ref, p1_ref, p1n_ref, ys_hbm, x_ref, gt_ref, mod_ref, nf_ref, o_ref,
                    b0, b1, sem0, sem1, *, final_norm):
    slot = _prefetched_gather([p0_ref, p1_ref], [p0n_ref, p1n_ref], ys_hbm, [b0, b1], [sem0, sem1])
    gt = gt_ref[...]
    y = gt[:, 0:1] * b0[slot] + gt[:, 1:2] * b1[slot]
    xn = x_ref[...] + mod_ref[0, 5:6, :] * y
    if final_norm:
        xn = _rms(xn) * nf_ref[...]
    o_ref[...] = xn


def _moe_combine(ys, pos0, pos1, x2, gates, mod, norm_final, l, L, final_norm):
    T, D = x2.shape
    rt = GATHER_ROWS
    per_b = L // rt
    cur, nxt = _idx_specs(T // rt, rt)
    p0 = pos0.reshape(T // rt, 1, rt)
    p1 = pos1.reshape(T // rt, 1, rt)
    return pl.pallas_call(
        functools.partial(_combine_kernel, final_norm=final_norm),
        out_shape=jax.ShapeDtypeStruct((T, D), f32),
        grid=(T // rt,),
        in_specs=[cur, nxt, cur, nxt, pl.BlockSpec(memory_space=pl.ANY),
                  pl.BlockSpec((rt, D), lambda i: (i, 0)),
                  pl.BlockSpec((rt, LANE), lambda i: (i, 0)),
                  pl.BlockSpec((None, 1, 6, D), _mod_spec(l, per_b, 1)),
                  pl.BlockSpec((1, D), lambda i: (0, 0))],
        out_specs=pl.BlockSpec((rt, D), lambda i: (i, 0)),
        scratch_shapes=[pltpu.VMEM((2, rt, D), f32), pltpu.VMEM((2, rt, D), f32),
                        pltpu.SemaphoreType.DMA((2,)), pltpu.SemaphoreType.DMA((2,))],
        compiler_params=_cparams(("arbitrary",), 40),
        name="moe_combine",
    )(p0, p0, p1, p1, ys, x2, gates, mod, norm_final)


def _moe_plan(route, T):
    tm = MOE_TM
    P = 2 * T + N_EXPERTS * tm
    e_flat = route[:, 2:4].astype(i32).reshape(-1)
    onehot = (e_flat[:, None] == jnp.arange(N_EXPERTS, dtype=i32)[None, :]).astype(i32)
    csum = jnp.cumsum(onehot, axis=0)
    counts = csum[-1]
    rank = jnp.sum(csum * onehot, axis=1) - 1
    tiles = (counts + tm - 1) // tm
    tile_end = jnp.cumsum(tiles)
    off = (tile_end - tiles) * tm
    pos = off[e_flat] + rank
    tok = jnp.arange(2 * T, dtype=i32) // 2
    src = jnp.zeros((P,), i32).at[pos].set(tok, unique_indices=True)
    n_used = tile_end[-1:]
    tidx = jnp.arange(P // tm, dtype=i32)
    te = jnp.sum((tidx[:, None] >= tile_end[None, :]).astype(i32), axis=1)
    te_last = jnp.sum((n_used - 1 >= tile_end).astype(i32))
    te = jnp.where(tidx < n_used, jnp.minimum(te, N_EXPERTS - 1), te_last).astype(i32)
    pos2 = pos.reshape(T, 2)
    return src, pos2[:, 0], pos2[:, 1], te, n_used.astype(i32)


def _permute_w_in(w_in):
    depth, d, _ = w_in.shape
    pad = jnp.zeros((depth, d, PROJ_W - OFF_SM - 48), w_in.dtype)
    parts = [w_in[:, :, 0:3072], w_in[:, :, 4376:5144], w_in[:, :, 3584:4352], w_in[:, :, 3072:3584],
             w_in[:, :, 5168:6704], w_in[:, :, 4352:4376], w_in[:, :, 5144:5168], pad]
    return jnp.concatenate(parts, axis=-1).astype(bf16)


def _small_rows(gdn_p, ssm_p):
    depth = gdn_p.shape[0]
    row = jnp.zeros((depth, LANE), f32)
    row = row.at[:, SM_GA:SM_GA + 2 * GDN_HEADS].set(gdn_p.reshape(depth, -1).astype(f32))
    row = row.at[:, SM_DT:SM_DT + 2 * SSM_HEADS].set(ssm_p.reshape(depth, -1).astype(f32))
    return row.reshape(depth, 1, LANE)


def kernel(x, c, ada_w, ada_b, norm_mix, norm_ffn, norm_final, w_in, conv_w, conv_b, gdn_A_log, gdn_dt_bias, gdn_norm, ssm_A_log, ssm_dt_bias, ssm_D, ssm_norm, na_rpb, w_out, ffn_w1, ffn_w3, ffn_w2, moe_router, moe_w1, moe_w3, moe_w2):
    bsz, L, D = x.shape
    depth = w_in.shape[0]
    T = bsz * L

    mod = _ada_mod(c, ada_w, ada_b).reshape(depth, bsz, 6, D)
    w_in_p = _permute_w_in(w_in)
    w_out_b = w_out.astype(bf16)
    alog_rows = _small_rows(gdn_A_log, ssm_A_log)
    dtb_rows = _small_rows(gdn_dt_bias, ssm_dt_bias)
    dskip_rows = jnp.repeat(ssm_D.astype(f32), SSM_HEADDIM, axis=-1).reshape(depth, 1, SSM_WIDTH)
    bias_tabs = _na_bias_tables(na_rpb)
    ffn_w1b, ffn_w3b, ffn_w2b = ffn_w1.astype(bf16), ffn_w3.astype(bf16), ffn_w2.astype(bf16)
    wr_pad = jnp.pad(moe_router.astype(f32), ((0, 0), (0, 0), (0, LANE - N_EXPERTS)))

    x2 = x.reshape(T, D)
    for l in range(depth):
        is_moe = l % 2 == 1
        proj = _inproj(x2, mod, norm_mix[l].reshape(1, D), w_in_p, l, L).reshape(bsz, L, PROJ_W)
        prep = _mix_prep(proj, alog_rows[l], dtb_rows[l])
        cb = conv_b[l].reshape(1, CONV_CH)
        ya = _gdn_mixer(proj, prep, conv_w[l], cb, gdn_norm[l].reshape(1, LANE))
        yb = _ssd_mixer(proj, prep, conv_w[l], cb, dskip_rows[l])
        yc = _na_mixer(proj, bias_tabs, l)
        x2, h2 = _outproj(ya.reshape(T, GDN_WIDTH), yb.reshape(T, SSM_WIDTH), yc.reshape(T, NA_WIDTH),
                          ssm_norm[l].reshape(1, SSM_WIDTH), w_out_b, x2, mod, norm_ffn[l].reshape(1, D),
                          l, L, f32 if is_moe else bf16)
        m = l // 2
        if not is_moe:
            x2 = _ffn_dense(h2, ffn_w1b, ffn_w3b, ffn_w2b, m, x2, mod, l, L)
        else:
            route = _router(h2, wr_pad, m)
            src, pos0, pos1, te, n_used = _moe_plan(route, T)
            hs = _gather_rows(h2, src, bf16)
            ys = _moe_ffn(hs, moe_w1, moe_w3, moe_w2, m, te, n_used)
            x2 = _moe_combine(ys, pos0, pos1, x2, route, mod, norm_final.reshape(1, D), l, L,
                              final_norm=(l == depth - 1))
    if depth % 2 == 1:
        raise NotImplementedError("the final norm is fused into the last MoE combine")
    return x2.reshape(bsz, L, D)
```
